```python
import jax, jax.numpy as jnp
from jax import lax
import numpy as np

D_MODEL = 2048
BATCH = 2
SEQ = 8192
DEPTH = 1
DEC_BATCH = 8
DEC_SEQ = 2048
PAST_LEN = 128

N_HEADS = 16
N_KV_HEADS = 4
HEAD_DIM = 128
WINDOW = 128
ATTN_BLOCK = 128
D_INNER = 2048
SSD_HEAD_DIM = 64
SSD_HEADS = D_INNER // SSD_HEAD_DIM
SSD_GROUPS = 4
D_STATE = 128
CONV_K = 5
SSD_CHUNK = 128
C_XBC = D_INNER + 2 * SSD_GROUPS * D_STATE
N_EXPERTS = 32
TOP_K = 4
D_FF = D_MODEL
SWIGLU_LIMIT = 7.0
SWIGLU_ALPHA = 1.702
MOE_BLOCK = 256
NORM_EPS = 1e-6

IN_SPLITS = (N_HEADS * HEAD_DIM, N_KV_HEADS * HEAD_DIM, N_KV_HEADS * HEAD_DIM, D_INNER, C_XBC, 2 * SSD_HEADS, D_MODEL, D_MODEL)
N_IN = sum(IN_SPLITS)

kernel_name = 'hybrid_bidir_swa_ssd_moe_encoder'


def rms_norm(x, w):
    xf = x.astype(jnp.float32)
    y = xf * lax.rsqrt(jnp.mean(xf * xf, axis=-1, keepdims=True) + NORM_EPS)
    return (y * w.astype(jnp.float32)).astype(x.dtype)


def alibi_slopes():
    return jnp.asarray(2.0 ** (-8.0 * (np.arange(N_HEADS, dtype=np.float32) + 1.0) / N_HEADS), jnp.float32)


def banded_attention(q, k, v, sink):
    b, S = q.shape[0], q.shape[1]
    nb = S // ATTN_BLOCK
    R = N_HEADS // N_KV_HEADS
    qb = q.reshape(b, nb, ATTN_BLOCK, N_KV_HEADS, R, HEAD_DIM)

    def band(t):
        tp = jnp.pad(t, ((0, 0), (ATTN_BLOCK, ATTN_BLOCK), (0, 0), (0, 0)))
        tp = tp.reshape(b, nb + 2, ATTN_BLOCK, N_KV_HEADS, HEAD_DIM)
        return jnp.concatenate([tp[:, :-2], tp[:, 1:-1], tp[:, 2:]], axis=2)

    kb, vb = band(k), band(v)
    s = jnp.einsum('bnqgrd,bnkgd->bngrqk', qb, kb).astype(jnp.float32) * (1.0 / np.sqrt(HEAD_DIM))
    qi = jnp.arange(ATTN_BLOCK)[:, None]
    kj = jnp.arange(3 * ATTN_BLOCK)[None, :]
    dist = jnp.abs(ATTN_BLOCK + qi - kj)
    kpos = (jnp.arange(nb)[:, None] - 1) * ATTN_BLOCK + kj
    valid = (dist <= WINDOW)[None] & ((kpos >= 0) & (kpos < S))[:, None, :]
    slopes = alibi_slopes().reshape(N_KV_HEADS, R)
    s = s - slopes[:, :, None, None] * dist.astype(jnp.float32)
    s = jnp.where(valid[None, :, None, None], s, -1e30)
    sink_l = sink.astype(jnp.float32).reshape(1, 1, N_KV_HEADS, R, 1, 1)
    m = jnp.maximum(jnp.max(s, axis=-1, keepdims=True), sink_l)
    p = jnp.exp(s - m)
    p = p / (jnp.sum(p, axis=-1, keepdims=True) + jnp.exp(sink_l - m))
    o = jnp.einsum('bngrqk,bnkgd->bnqgrd', p.astype(v.dtype), vb)
    return o.reshape(b, S, N_HEADS * HEAD_DIM)


def ssd_chunked(x, dt, A, Bm, Cm):
    b, L = x.shape[0], x.shape[1]
    c = L // SSD_CHUNK
    G, R = SSD_GROUPS, SSD_HEADS // SSD_GROUPS
    x = x.reshape(b, c, SSD_CHUNK, G, R, SSD_HEAD_DIM)
    dt = dt.reshape(b, c, SSD_CHUNK, G, R)
    Bm = Bm.reshape(b, c, SSD_CHUNK, G, D_STATE)
    Cm = Cm.reshape(b, c, SSD_CHUNK, G, D_STATE)
    a_cum = jnp.cumsum(dt * A.reshape(G, R), axis=2)
    lower = jnp.tril(jnp.ones((SSD_CHUNK, SSD_CHUNK), bool))
    diff = a_cum[:, :, :, None] - a_cum[:, :, None]
    decay = jnp.exp(jnp.where(lower[:, :, None, None], diff, -jnp.inf))
    dtx = x * dt[..., None]
    cb = jnp.einsum('bclgn,bcsgn->bclsg', Cm, Bm)
    y_diag = jnp.einsum('bclsg,bclsgr,bcsgrp->bclgrp', cb, decay, dtx)
    decay_states = jnp.exp(a_cum[:, :, -1:] - a_cum)
    states = jnp.einsum('bclgn,bclgr,bclgrp->bcgrpn', Bm, decay_states, dtx)
    chunk_decay = jnp.exp(a_cum[:, :, -1])

    def step(prev, inp):
        st, dec = inp
        return prev * dec[..., None, None] + st, prev

    init = jnp.zeros((b, G, R, SSD_HEAD_DIM, D_STATE), jnp.float32)
    _, prev_states = lax.scan(step, init, (jnp.swapaxes(states, 0, 1), jnp.swapaxes(chunk_decay, 0, 1)))
    prev_states = jnp.swapaxes(prev_states, 0, 1)
    y_off = jnp.einsum('bclgn,bcgrpn,bclgr->bclgrp', Cm, prev_states, jnp.exp(a_cum))
    return (y_diag + y_off).reshape(b, L, SSD_HEADS, SSD_HEAD_DIM)


def mixer_sublayer(x, norm1_w, w_in, q_norm_w, k_norm_w, attn_sink, conv_w, conv_b, dt_bias, a_log, d_skip,
                   ssd_norm_w, w_attn_proj, w_ssd_proj, w_out):
    b, S, _ = x.shape
    h = rms_norm(x, norm1_w)
    proj = h @ w_in
    offs = [int(o) for o in np.cumsum(IN_SPLITS)[:-1]]
    q, k, v, z, xbc, dt_raw, g_attn, g_ssd = jnp.split(proj, offs, axis=-1)
    q = rms_norm(q.reshape(b, S, N_HEADS, HEAD_DIM), q_norm_w)
    k = rms_norm(k.reshape(b, S, N_KV_HEADS, HEAD_DIM), k_norm_w)
    v = v.reshape(b, S, N_KV_HEADS, HEAD_DIM)
    attn = banded_attention(q, k, v, attn_sink)
    xbc = lax.conv_general_dilated(xbc, conv_w[:, None, :].astype(xbc.dtype), window_strides=(1,),
                                   padding=[(CONV_K // 2, CONV_K // 2)], dimension_numbers=('NWC', 'WIO', 'NWC'),
                                   feature_group_count=C_XBC)
    xbc = jax.nn.silu(xbc + conv_b)
    xs, Bm, Cm = jnp.split(xbc, [D_INNER, D_INNER + SSD_GROUPS * D_STATE], axis=-1)
    xs = xs.reshape(b, S, SSD_HEADS, SSD_HEAD_DIM).astype(jnp.float32)
    Bm = Bm.reshape(b, S, SSD_GROUPS, D_STATE).astype(jnp.float32)
    Cm = Cm.reshape(b, S, SSD_GROUPS, D_STATE).astype(jnp.float32)
    dt = jax.nn.softplus(dt_raw.astype(jnp.float32).reshape(b, S, 2, SSD_HEADS) + dt_bias.astype(jnp.float32))
    A = -jnp.exp(a_log.astype(jnp.float32))
    flip = lambda t: jnp.flip(t, axis=1)
    y_f = ssd_chunked(xs, dt[:, :, 0], A[0], Bm, Cm)
    y_b = flip(ssd_chunked(flip(xs), flip(dt[:, :, 1]), A[1], flip(Bm), flip(Cm)))
    y = y_f + y_b + xs * d_skip.astype(jnp.float32)[:, None]
    yg = (y.reshape(b, S, D_INNER) * jax.nn.silu(z.astype(jnp.float32))).reshape(b, S, SSD_GROUPS, D_INNER // SSD_GROUPS)
    yg = yg * lax.rsqrt(jnp.mean(yg * yg, axis=-1, keepdims=True) + NORM_EPS)
    ssd = (yg.reshape(b, S, D_INNER) * ssd_norm_w.astype(jnp.float32)).astype(x.dtype)
    merged = jax.nn.sigmoid(g_attn) * (attn @ w_attn_proj) + jax.nn.sigmoid(g_ssd) * (ssd @ w_ssd_proj)
    return x + merged @ w_out


def moe_ffn(h, router_w, router_b, w_gate_up, b_gate_up, w_down, b_down):
    T, D = h.shape
    logits = (h @ router_w + router_b).astype(jnp.float32)
    top_val, top_idx = lax.top_k(logits, TOP_K)
    gate = jax.nn.softmax(top_val, axis=-1)
    TK = T * TOP_K
    n_blocks = -(-TK // MOE_BLOCK) + N_EXPERTS
    flat_e = top_idx.reshape(-1)
    flat_tok = jnp.arange(TK, dtype=jnp.int32) // TOP_K
    flat_w = gate.reshape(-1)
    order = jnp.argsort(flat_e)
    sorted_e = flat_e[order]
    counts = jnp.bincount(flat_e, length=N_EXPERTS)
    start = jnp.cumsum(counts) - counts
    padded = ((counts + MOE_BLOCK - 1) // MOE_BLOCK) * MOE_BLOCK
    pend = jnp.cumsum(padded)
    pstart = pend - padded
    dest = pstart[sorted_e] + jnp.arange(TK) - start[sorted_e]
    buf_tok = jnp.full((n_blocks * MOE_BLOCK,), T, jnp.int32).at[dest].set(flat_tok[order])
    buf_w = jnp.zeros((n_blocks * MOE_BLOCK,), jnp.float32).at[dest].set(flat_w[order])
    block_e = jnp.minimum(jnp.searchsorted(pend, jnp.arange(n_blocks) * MOE_BLOCK, side='right'), N_EXPERTS - 1)
    h_pad = jnp.concatenate([h, jnp.zeros((1, D), h.dtype)], axis=0)

    def expert_block(args):
        e, tok, wt = args
        gu = h_pad[tok] @ w_gate_up[e] + b_gate_up[e]
        g, u = jnp.split(gu, 2, axis=-1)
        g = jnp.minimum(g, SWIGLU_LIMIT)
        u = jnp.clip(u, -SWIGLU_LIMIT, SWIGLU_LIMIT)
        act = g * jax.nn.sigmoid(SWIGLU_ALPHA * g) * (u + 1.0)
        return (act @ w_down[e] + b_down[e]) * wt[:, None].astype(h.dtype)

    outs = lax.map(expert_block, (block_e, buf_tok.reshape(n_blocks, MOE_BLOCK), buf_w.reshape(n_blocks, MOE_BLOCK)))
    y = jnp.zeros((T + 1, D), h.dtype).at[buf_tok].add(outs.reshape(-1, D))
    return y[:T]


def trunk(x, norm1_w, w_in, q_norm_w, k_norm_w, attn_sink, conv_w, conv_b, dt_bias, a_log, d_skip, ssd_norm_w,
          w_attn_proj, w_ssd_proj, w_out, norm2_w, router_w, router_b, w_gate_up, b_gate_up, w_down, b_down):
    b, S, D = x.shape
    for l in range(DEPTH):
        x = mixer_sublayer(x, norm1_w[l], w_in[l], q_norm_w[l], k_norm_w[l], attn_sink[l], conv_w[l], conv_b[l],
                           dt_bias[l], a_log[l], d_skip[l], ssd_norm_w[l], w_attn_proj[l], w_ssd_proj[l], w_out[l])
        h = rms_norm(x, norm2_w[l]).reshape(b * S, D)
        x = x + moe_ffn(h, router_w[l], router_b[l], w_gate_up[l], b_gate_up[l], w_down[l], b_down[l]).reshape(b, S, D)
    return x


def setup_inputs(seed: int = 0) -> dict:
    key = jax.random.key(seed)
    ks = jax.random.split(key, 24)
    f32 = jnp.float32
    nrm = lambda k, shape, scale: jax.random.normal(k, shape, f32) * scale
    dt0 = jnp.exp(jax.random.uniform(ks[9], (DEPTH, 2, SSD_HEADS), f32, np.log(1e-3), np.log(1e-1)))
    return {
        'x_prompt': jax.random.normal(ks[0], (BATCH, SEQ, D_MODEL), f32),
        'x_sample': jax.random.normal(ks[1], (DEC_BATCH, DEC_SEQ, D_MODEL), f32),
        'norm1_w': 1.0 + nrm(ks[2], (DEPTH, D_MODEL), 0.02),
        'w_in': nrm(ks[3], (DEPTH, D_MODEL, N_IN), D_MODEL ** -0.5),
        'q_norm_w': 1.0 + nrm(ks[4], (DEPTH, HEAD_DIM), 0.02),
        'k_norm_w': 1.0 + nrm(ks[5], (DEPTH, HEAD_DIM), 0.02),
        'attn_sink': nrm(ks[6], (DEPTH, N_HEADS), 0.5),
        'conv_w': nrm(ks[7], (DEPTH, CONV_K, C_XBC), CONV_K ** -0.5),
        'conv_b': nrm(ks[8], (DEPTH, C_XBC), 0.02),
        'dt_bias': dt0 + jnp.log(-jnp.expm1(-dt0)),
        'a_log': jnp.log(jax.random.uniform(ks[10], (DEPTH, 2, SSD_HEADS), f32, 1.0, 16.0)),
        'd_skip': 1.0 + nrm(ks[11], (DEPTH, SSD_HEADS), 0.02),
        'ssd_norm_w': 1.0 + nrm(ks[12], (DEPTH, D_INNER), 0.02),
        'w_attn_proj': nrm(ks[13], (DEPTH, N_HEADS * HEAD_DIM, D_MODEL), (N_HEADS * HEAD_DIM) ** -0.5),
        'w_ssd_proj': nrm(ks[14], (DEPTH, D_INNER, D_MODEL), D_INNER ** -0.5),
        'w_out': nrm(ks[15], (DEPTH, D_MODEL, D_MODEL), D_MODEL ** -0.5),
        'norm2_w': 1.0 + nrm(ks[16], (DEPTH, D_MODEL), 0.02),
        'router_w': nrm(ks[17], (DEPTH, D_MODEL, N_EXPERTS), D_MODEL ** -0.5),
        'router_b': nrm(ks[18], (DEPTH, N_EXPERTS), 0.01),
        'w_gate_up': nrm(ks[19], (DEPTH, N_EXPERTS, D_MODEL, 2 * D_FF), D_MODEL ** -0.5),
        'b_gate_up': nrm(ks[20], (DEPTH, N_EXPERTS, 2 * D_FF), 0.01),
        'w_down': nrm(ks[21], (DEPTH, N_EXPERTS, D_FF, D_MODEL), D_FF ** -0.5),
        'b_down': nrm(ks[22], (DEPTH, N_EXPERTS, D_MODEL), 0.01),
    }


def reference(x_prompt, x_sample, norm1_w, w_in, q_norm_w, k_norm_w, attn_sink, conv_w, conv_b, dt_bias, a_log,
              d_skip, ssd_norm_w, w_attn_proj, w_ssd_proj, w_out, norm2_w, router_w, router_b, w_gate_up, b_gate_up,
              w_down, b_down):
    y_prompt = trunk(x_prompt, norm1_w, w_in, q_norm_w, k_norm_w, attn_sink, conv_w, conv_b, dt_bias, a_log, d_skip,
                     ssd_norm_w, w_attn_proj, w_ssd_proj, w_out, norm2_w, router_w, router_b, w_gate_up, b_gate_up,
                     w_down, b_down)
    y_sample = trunk(x_sample, norm1_w, w_in, q_norm_w, k_norm_w, attn_sink, conv_w, conv_b, dt_bias, a_log, d_skip,
                     ssd_norm_w, w_attn_proj, w_ssd_proj, w_out, norm2_w, router_w, router_b, w_gate_up, b_gate_up,
                     w_down, b_down)
    return (y_prompt, y_sample)
```

```python
import functools

import numpy as np
import jax
import jax.numpy as jnp
from jax import lax
from jax.experimental import pallas as pl
from jax.experimental.pallas import tpu as pltpu

F32 = jnp.float32
BF16 = jnp.bfloat16
HIGHEST = lax.Precision.HIGHEST

D_MODEL = 2048
N_HEADS = 16
N_KV_HEADS = 4
Q_PER_KV = N_HEADS // N_KV_HEADS
HEAD_DIM = 128
WINDOW = 128
ATTN_BLOCK = 128
D_INNER = 2048
SSD_HEAD_DIM = 64
SSD_HEADS = D_INNER // SSD_HEAD_DIM
SSD_GROUPS = 4
HEADS_PER_GROUP = SSD_HEADS // SSD_GROUPS
GROUP_WIDTH = D_INNER // SSD_GROUPS
D_STATE = 128
CONV_K = 5
SSD_CHUNK = 128
C_XBC = D_INNER + 2 * SSD_GROUPS * D_STATE
N_EXPERTS = 32
TOP_K = 4
D_FF = D_MODEL
SWIGLU_LIMIT = 7.0
SWIGLU_ALPHA = 1.702
NORM_EPS = 1e-6
MASK_VALUE = -1e30

LANES = 128
BF16_SUBLANES = 16
MIB = 1 << 20


def _cparams(semantics, vmem_mib):
    return pltpu.CompilerParams(dimension_semantics=semantics, vmem_limit_bytes=vmem_mib * MIB)


def _sigmoid(x):
    return 1.0 / (1.0 + jnp.exp(-x))


def _pick_tile(total, preferred):
    t = min(total, preferred)
    while total % t:
        t //= 2
    return t


def _norm1_kernel(xp_ref, xs_ref, w_ref, o_ref, *, n_prompt_tiles):
    m = pl.program_id(0)

    def body(x_ref):
        x = x_ref[...]
        ms = jnp.mean(x * x, axis=-1, keepdims=True)
        o_ref[...] = (x * lax.rsqrt(ms + NORM_EPS) * w_ref[...]).astype(BF16)

    @pl.when(m < n_prompt_tiles)
    def _():
        body(xp_ref)

    @pl.when(m >= n_prompt_tiles)
    def _():
        body(xs_ref)


def _two_group_specs(tile, n_prompt_tiles, n_sample_tiles, width):
    p_spec = pl.BlockSpec((tile, width), lambda m: (jnp.minimum(m, n_prompt_tiles - 1), 0))
    s_spec = pl.BlockSpec((tile, width), lambda m: (jnp.maximum(m - n_prompt_tiles, 0), 0))
    return p_spec, s_spec


def _norm1(xp, xs, w):
    tp, ts = xp.shape[0], xs.shape[0]
    tile = _pick_tile(int(np.gcd(tp, ts)), 512)
    npt, nst = tp // tile, ts // tile
    p_spec, s_spec = _two_group_specs(tile, npt, nst, D_MODEL)
    return pl.pallas_call(
        functools.partial(_norm1_kernel, n_prompt_tiles=npt),
        grid=(npt + nst,),
        in_specs=[p_spec, s_spec, pl.BlockSpec((1, D_MODEL), lambda m: (0, 0))],
        out_specs=pl.BlockSpec((tile, D_MODEL), lambda m: (m, 0)),
        out_shape=jax.ShapeDtypeStruct((tp + ts, D_MODEL), BF16),
        compiler_params=_cparams(("parallel",), 40),
        name="norm1",
    )(xp, xs, w.reshape(1, D_MODEL))


def _linear_kernel(*refs, n_lhs, n_aux, epilogue):
    lhs = refs[:n_lhs]
    rhs = refs[n_lhs:2 * n_lhs]
    aux = refs[2 * n_lhs:2 * n_lhs + n_aux]
    out = refs[2 * n_lhs + n_aux]
    accs = [jnp.dot(l[...], r[...], preferred_element_type=F32) for l, r in zip(lhs, rhs)]
    epilogue(accs, aux, out)


def _ep_cast(accs, aux, out):
    out[...] = accs[0].astype(out.dtype)


def _ep_silu(accs, aux, out):
    a = accs[0]
    out[...] = (a * _sigmoid(a)).astype(out.dtype)


def _ep_sigmoid(accs, aux, out):
    out[...] = _sigmoid(accs[0]).astype(out.dtype)


def _ep_head_norm(accs, aux, out):
    a = accs[0]
    w = aux[0][...]
    for j in range(a.shape[1] // HEAD_DIM):
        s = a[:, j * HEAD_DIM:(j + 1) * HEAD_DIM]
        ms = jnp.mean(s * s, axis=-1, keepdims=True)
        out[:, j * HEAD_DIM:(j + 1) * HEAD_DIM] = (s * lax.rsqrt(ms + NORM_EPS) * w).astype(out.dtype)


def _ep_gated_sum(accs, aux, out):
    out[...] = (aux[0][...].astype(F32) * accs[0] + aux[1][...].astype(F32) * accs[1]).astype(out.dtype)


def _linear(lhs_list, rhs_list, aux_list, aux_specs, epilogue, out_dtype, tm, tn, name):
    t, k = lhs_list[0].shape
    n = rhs_list[0].shape[1]
    tm = _pick_tile(t, tm)
    tn = _pick_tile(n, tn)
    in_specs = ([pl.BlockSpec((tm, k), lambda m, j: (m, 0)) for _ in lhs_list]
                + [pl.BlockSpec((k, tn), lambda m, j: (0, j)) for _ in rhs_list]
                + list(aux_specs(tm, tn)))
    return pl.pallas_call(
        functools.partial(_linear_kernel, n_lhs=len(lhs_list), n_aux=len(aux_list), epilogue=epilogue),
        grid=(t // tm, n // tn),
        in_specs=in_specs,
        out_specs=pl.BlockSpec((tm, tn), lambda m, j: (m, j)),
        out_shape=jax.ShapeDtypeStruct((t, n), out_dtype),
        compiler_params=_cparams(("parallel", "arbitrary"), 48),
        name=name,
    )(*lhs_list, *rhs_list, *aux_list)


def _no_aux(tm, tn):
    return []


def _attn_kernel(flags_ref, slope_ref, sink_ref, q_ref, k_ref, kp_ref, kn_ref, v_ref, vp_ref, vn_ref, o_ref,
                 *, n_sub):
    i = pl.program_id(0)
    g = pl.program_id(1)
    has_prev = flags_ref[0, i] == 0
    has_next = flags_ref[1, i] == 0

    qi = lax.broadcasted_iota(jnp.int32, (ATTN_BLOCK, 3 * ATTN_BLOCK), 0)
    kj = lax.broadcasted_iota(jnp.int32, (ATTN_BLOCK, 3 * ATTN_BLOCK), 1)
    dist = jnp.abs(ATTN_BLOCK + qi - kj)
    in_window = dist <= WINDOW
    dist_f = dist.astype(F32)
    is_prev_blk = kj < ATTN_BLOCK
    is_next_blk = kj >= 2 * ATTN_BLOCK

    for j in range(n_sub):
        rows = slice(j * ATTN_BLOCK, (j + 1) * ATTN_BLOCK)
        prev_rows = slice((j - 1) * ATTN_BLOCK, j * ATTN_BLOCK)
        next_rows = slice((j + 1) * ATTN_BLOCK, (j + 2) * ATTN_BLOCK)
        k_prev = kp_ref[...] if j == 0 else k_ref[prev_rows, :]
        v_prev = vp_ref[...] if j == 0 else v_ref[prev_rows, :]
        k_next = kn_ref[...] if j == n_sub - 1 else k_ref[next_rows, :]
        v_next = vn_ref[...] if j == n_sub - 1 else v_ref[next_rows, :]
        k_band = jnp.concatenate([k_prev, k_ref[rows, :], k_next], axis=0)
        v_band = jnp.concatenate([v_prev, v_ref[rows, :], v_next], axis=0)
        valid = in_window
        if j == 0:
            valid = valid & (has_prev | jnp.logical_not(is_prev_blk))
        if j == n_sub - 1:
            valid = valid & (has_next | jnp.logical_not(is_next_blk))
        q_stack = jnp.concatenate(
            [q_ref[rows, r * HEAD_DIM:(r + 1) * HEAD_DIM] for r in range(Q_PER_KV)], axis=0)
        s_all = lax.dot_general(q_stack, k_band, (((1,), (1,)), ((), ())), preferred_element_type=F32)
        for r in range(Q_PER_KV):
            head = g * Q_PER_KV + r
            s = s_all[r * ATTN_BLOCK:(r + 1) * ATTN_BLOCK, :] - slope_ref[head] * dist_f
            s = jnp.where(valid, s, MASK_VALUE)
            sink = sink_ref[head]
            m = jnp.maximum(jnp.max(s, axis=-1, keepdims=True), sink)
            p = jnp.exp(s - m)
            denom = jnp.sum(p, axis=-1, keepdims=True) + jnp.exp(sink - m)
            o = jnp.dot(p.astype(BF16), v_band, preferred_element_type=F32)
            o_ref[rows, r * HEAD_DIM:(r + 1) * HEAD_DIM] = (o / denom).astype(o_ref.dtype)


def _attention(q, k, v, sink, chunk_flags, tq):
    t = q.shape[0]
    n_chunks = t // tq
    n_blocks = t // ATTN_BLOCK
    sub = tq // ATTN_BLOCK
    slopes = jnp.asarray(2.0 ** (-8.0 * (np.arange(N_HEADS, dtype=np.float32) + 1.0) / N_HEADS), F32)
    gw = Q_PER_KV * HEAD_DIM

    def own(width):
        return pl.BlockSpec((tq, width), lambda i, g, *_: (i, g))

    prev = pl.BlockSpec((ATTN_BLOCK, HEAD_DIM), lambda i, g, *_: (jnp.maximum(i * sub - 1, 0), g))
    nxt = pl.BlockSpec((ATTN_BLOCK, HEAD_DIM), lambda i, g, *_: (jnp.minimum((i + 1) * sub, n_blocks - 1), g))
    grid_spec = pltpu.PrefetchScalarGridSpec(
        num_scalar_prefetch=3,
        grid=(n_chunks, N_KV_HEADS),
        in_specs=[own(gw), own(HEAD_DIM), prev, nxt, own(HEAD_DIM), prev, nxt],
        out_specs=own(gw),
    )
    return pl.pallas_call(
        functools.partial(_attn_kernel, n_sub=sub),
        grid_spec=grid_spec,
        out_shape=jax.ShapeDtypeStruct((t, N_HEADS * HEAD_DIM), BF16),
        compiler_params=_cparams(("parallel", "arbitrary"), 32),
        name="banded_attention",
    )(chunk_flags, slopes, sink.astype(F32), q, k, k, k, v, v, v)


CONV_HALO = BF16_SUBLANES
CONV_PAD = CONV_K // 2
CONV_TAP_ROWS = 8
CONV_COL_TILE = 512
HEAD_PAIR_WIDTH = 2 * SSD_HEAD_DIM


def _ssd_kernel(flags_ref, xbc_ref, xprev_ref, xnext_ref, dt_ref, convw_ref, convb_ref, dtb_ref, alog_ref, *rest,
                reverse):
    if reverse:
        yf_ref, sz_ref, dskip_ref, normw_ref, out_ref, ext_scr, xc_scr, y_scr, state_scr = rest
    else:
        out_ref, ext_scr, xc_scr, y_scr, state_scr = rest
    i = pl.program_id(0)
    c = pl.num_programs(0) - 1 - i if reverse else i
    seq_first = flags_ref[0, c] == 1
    seq_last = flags_ref[1, c] == 1
    L = SSD_CHUNK

    ext_scr[0:CONV_HALO, :] = jnp.where(seq_first, 0.0, xprev_ref[...].astype(F32))
    ext_scr[CONV_HALO:CONV_HALO + L, :] = xbc_ref[...].astype(F32)
    ext_scr[CONV_HALO + L:, :] = jnp.where(seq_last, 0.0, xnext_ref[...].astype(F32))
    for ct in range(C_XBC // CONV_COL_TILE):
        cols = slice(ct * CONV_COL_TILE, (ct + 1) * CONV_COL_TILE)
        acc = jnp.broadcast_to(convb_ref[:, cols], (L, CONV_COL_TILE))
        for j in range(CONV_K):
            r0 = CONV_HALO - CONV_PAD + j
            acc = acc + ext_scr[r0:r0 + L, cols] * convw_ref[j:j + 1, cols]
        xc_scr[:, cols] = acc * _sigmoid(acc)

    col = lax.broadcasted_iota(jnp.int32, (1, LANES), 1)
    a_neg = jnp.where(col < 2 * SSD_HEADS, -jnp.exp(alog_ref[...]), 0.0)
    xdt = dt_ref[...] + dtb_ref[...]
    dt = jnp.maximum(xdt, 0.0) + jnp.log1p(jnp.exp(-jnp.abs(xdt)))
    a = dt * a_neg
    ri = lax.broadcasted_iota(jnp.int32, (L, L), 0)
    ci = lax.broadcasted_iota(jnp.int32, (L, L), 1)
    causal = (ri <= ci) if reverse else (ri >= ci)
    a_cum = jnp.dot(causal.astype(F32), a, precision=HIGHEST, preferred_element_type=F32)
    a_cum_t = a_cum.T
    dt_t = dt.T
    edge = 0 if reverse else L - 1
    a_total = a_cum[edge:edge + 1, :]
    w_state = dt * jnp.exp(a_total - a_cum)
    dir_off = SSD_HEADS if reverse else 0
    hr = lax.broadcasted_iota(jnp.int32, (LANES, D_INNER), 0)
    hc = lax.broadcasted_iota(jnp.int32, (LANES, D_INNER), 1)
    expand = (hr == dir_off + lax.shift_right_logical(hc, int(np.log2(SSD_HEAD_DIM)))).astype(F32)
    chunk_decay = jnp.dot(jnp.broadcast_to(jnp.exp(a_total), (8, LANES)), expand, precision=HIGHEST,
                          preferred_element_type=F32)[0:1, :]

    @pl.when(seq_last if reverse else seq_first)
    def _():
        state_scr[...] = jnp.zeros_like(state_scr)

    lo = lax.broadcasted_iota(jnp.int32, (L, LANES), 1) < SSD_HEAD_DIM
    for g in range(SSD_GROUPS):
        b_g = xc_scr[:, D_INNER + g * D_STATE:D_INNER + (g + 1) * D_STATE]
        c_g = xc_scr[:, D_INNER + (SSD_GROUPS + g) * D_STATE:D_INNER + (SSD_GROUPS + g + 1) * D_STATE]
        c_bf = c_g.astype(BF16)
        cb = lax.dot_general(c_bf, b_g.astype(BF16), (((1,), (1,)), ((), ())), preferred_element_type=F32)
        b_t = b_g.T.astype(BF16)
        state = state_scr[g]
        y_off = jnp.dot(c_bf, state.astype(BF16), preferred_element_type=F32)
        xw_parts = []
        for p in range(HEADS_PER_GROUP // 2):
            gcols = slice(g * GROUP_WIDTH + p * HEAD_PAIR_WIDTH, g * GROUP_WIDTH + (p + 1) * HEAD_PAIR_WIDTH)
            m_parts, e_parts, w_parts = [], [], []
            for hh in range(2):
                k = dir_off + g * HEADS_PER_GROUP + 2 * p + hh
                colb = jnp.broadcast_to(a_cum[:, k:k + 1], (L, L))
                rowb = jnp.broadcast_to(a_cum_t[k:k + 1, :], (L, L))
                dtrow = jnp.broadcast_to(dt_t[k:k + 1, :], (L, L))
                decay = jnp.exp(jnp.where(causal, colb - rowb, -jnp.inf))
                m_parts.append((cb * decay * dtrow).astype(BF16))
                e_parts.append(jnp.exp(colb))
                w_parts.append(jnp.broadcast_to(w_state[:, k:k + 1], (L, LANES)))
            x_pair = xc_scr[:, gcols]
            rhs = jnp.concatenate([jnp.where(lo, x_pair, 0.0), jnp.where(lo, 0.0, x_pair)], axis=0).astype(BF16)
            y = jnp.dot(jnp.concatenate(m_parts, axis=1), rhs, preferred_element_type=F32)
            y = y + y_off[:, p * HEAD_PAIR_WIDTH:(p + 1) * HEAD_PAIR_WIDTH] * jnp.where(lo, e_parts[0], e_parts[1])
            y_scr[:, gcols] = y
            xw_parts.append((x_pair * jnp.where(lo, w_parts[0], w_parts[1])).astype(BF16))
        xw = jnp.concatenate(xw_parts, axis=1)
        state_scr[g] = (state * chunk_decay[:, g * GROUP_WIDTH:(g + 1) * GROUP_WIDTH]
                        + jnp.dot(b_t, xw, preferred_element_type=F32))

    if not reverse:
        out_ref[...] = y_scr[...].astype(out_ref.dtype)
    else:
        for g in range(SSD_GROUPS):
            cols = slice(g * GROUP_WIDTH, (g + 1) * GROUP_WIDTH)
            y = y_scr[:, cols] + yf_ref[:, cols].astype(F32) + xc_scr[:, cols] * dskip_ref[:, cols]
            yg = y * sz_ref[:, cols].astype(F32)
            ms = jnp.mean(yg * yg, axis=-1, keepdims=True)
            out_ref[:, cols] = (yg * lax.rsqrt(ms + NORM_EPS) * normw_ref[:, cols]).astype(out_ref.dtype)


def _ssd_pass(chunk_flags, xbc, dt, conv_w, conv_b, dt_bias, a_log, reverse, extras=()):
    t = xbc.shape[0]
    n_chunks = t // SSD_CHUNK
    halo_per_chunk = SSD_CHUNK // CONV_HALO
    n_halo_blocks = t // CONV_HALO

    def cidx(i):
        return n_chunks - 1 - i if reverse else i

    def row(width, rows=SSD_CHUNK):
        return pl.BlockSpec((rows, width), lambda i, *_: (cidx(i), 0))

    def const(shape):
        return pl.BlockSpec(shape, lambda i, *_: (0, 0))

    prev = pl.BlockSpec((CONV_HALO, C_XBC), lambda i, *_: (jnp.maximum(cidx(i) * halo_per_chunk - 1, 0), 0))
    nxt = pl.BlockSpec((CONV_HALO, C_XBC),
                       lambda i, *_: (jnp.minimum((cidx(i) + 1) * halo_per_chunk, n_halo_blocks - 1), 0))
    in_specs = [row(C_XBC), prev, nxt, row(LANES), const((CONV_TAP_ROWS, C_XBC)), const((1, C_XBC)),
                const((1, LANES)), const((1, LANES))]
    if reverse:
        in_specs += [row(D_INNER), row(D_INNER), const((1, D_INNER)), const((1, D_INNER))]
    grid_spec = pltpu.PrefetchScalarGridSpec(
        num_scalar_prefetch=1,
        grid=(n_chunks,),
        in_specs=in_specs,
        out_specs=row(D_INNER),
        scratch_shapes=[
            pltpu.VMEM((SSD_CHUNK + 2 * CONV_HALO, C_XBC), F32),
            pltpu.VMEM((SSD_CHUNK, C_XBC), F32),
            pltpu.VMEM((SSD_CHUNK, D_INNER), F32),
            pltpu.VMEM((SSD_GROUPS, D_STATE, GROUP_WIDTH), F32),
        ],
    )
    pad = LANES - 2 * SSD_HEADS
    return pl.pallas_call(
        functools.partial(_ssd_kernel, reverse=reverse),
        grid_spec=grid_spec,
        out_shape=jax.ShapeDtypeStruct((t, D_INNER), BF16),
        compiler_params=_cparams(("arbitrary",), 40),
        name="ssd_bwd" if reverse else "ssd_fwd",
    )(chunk_flags, xbc, xbc, xbc, dt,
      jnp.pad(conv_w.astype(F32), ((0, CONV_TAP_ROWS - CONV_K), (0, 0))), conv_b.astype(F32).reshape(1, C_XBC),
      jnp.pad(dt_bias.astype(F32).reshape(1, -1), ((0, 0), (0, pad))),
      jnp.pad(a_log.astype(F32).reshape(1, -1), ((0, 0), (0, pad))), *extras)


def _ssd(proj, seq_lengths, conv_w, conv_b, dt_bias, a_log, d_skip, ssd_norm_w):
    flags = _sequence_flags(seq_lengths, SSD_CHUNK)
    args = (flags, proj['xbc'], proj['dt'], conv_w, conv_b, dt_bias, a_log)
    y_fwd = _ssd_pass(*args, reverse=False)
    d_lanes = jnp.repeat(d_skip.astype(F32), SSD_HEAD_DIM).reshape(1, D_INNER)
    return _ssd_pass(*args, reverse=True,
                     extras=(y_fwd, proj['silu_z'], d_lanes, ssd_norm_w.astype(F32).reshape(1, D_INNER)))


def _out_router_kernel(m_ref, w_ref, xp_ref, xs_ref, nw_ref, rw_ref, rb_ref, x1_ref, h2_ref, ti_ref, tw_ref,
                       *, n_prompt_tiles):
    i = pl.program_id(0)
    acc = jnp.dot(m_ref[...], w_ref[...], preferred_element_type=F32)

    def finish(x_ref):
        x1 = x_ref[...] + acc
        x1_ref[...] = x1
        ms = jnp.mean(x1 * x1, axis=-1, keepdims=True)
        h2 = x1 * lax.rsqrt(ms + NORM_EPS) * nw_ref[...]
        h2_ref[...] = h2
        v = jnp.dot(h2.astype(BF16), rw_ref[...], preferred_element_type=F32) + rb_ref[...]
        lane = lax.broadcasted_iota(jnp.int32, v.shape, 1)
        vals, idxs = [], []
        for _ in range(TOP_K):
            top = jnp.max(v, axis=-1, keepdims=True)
            idx = jnp.min(jnp.where(v == top, lane, LANES), axis=-1, keepdims=True)
            vals.append(top)
            idxs.append(idx)
            v = jnp.where(lane == idx, -jnp.inf, v)
        es = [jnp.exp(val - vals[0]) for val in vals]
        total = es[0]
        for e in es[1:]:
            total = total + e
        ti = jnp.zeros(v.shape, jnp.int32)
        tw = jnp.zeros(v.shape, F32)
        for kk in range(TOP_K):
            ti = jnp.where(lane == kk, idxs[kk], ti)
            tw = jnp.where(lane == kk, es[kk] / total, tw)
        ti_ref[...] = ti
        tw_ref[...] = tw

    @pl.when(i < n_prompt_tiles)
    def _():
        finish(xp_ref)

    @pl.when(i >= n_prompt_tiles)
    def _():
        finish(xs_ref)


def _out_router(merged, w_out, xp, xs, norm2_w, router_w, router_b):
    tp, ts = xp.shape[0], xs.shape[0]
    t = tp + ts
    tile = _pick_tile(int(np.gcd(tp, ts)), 256)
    npt, nst = tp // tile, ts // tile
    p_spec, s_spec = _two_group_specs(tile, npt, nst, D_MODEL)
    rw = jnp.pad(router_w.astype(BF16), ((0, 0), (0, LANES - N_EXPERTS)))
    rb = jnp.pad(router_b.astype(F32).reshape(1, N_EXPERTS), ((0, 0), (0, LANES - N_EXPERTS)),
                 constant_values=-jnp.inf)

    def const(shape):
        return pl.BlockSpec(shape, lambda m: (0, 0))

    def row(width):
        return pl.BlockSpec((tile, width), lambda m: (m, 0))

    return pl.pallas_call(
        functools.partial(_out_router_kernel, n_prompt_tiles=npt),
        grid=(npt + nst,),
        in_specs=[row(D_MODEL), const((D_MODEL, D_MODEL)), p_spec, s_spec, const((1, D_MODEL)),
                  const((D_MODEL, LANES)), const((1, LANES))],
        out_specs=[row(D_MODEL), row(D_MODEL), row(LANES), row(LANES)],
        out_shape=[jax.ShapeDtypeStruct((t, D_MODEL), F32), jax.ShapeDtypeStruct((t, D_MODEL), F32),
                   jax.ShapeDtypeStruct((t, LANES), jnp.int32), jax.ShapeDtypeStruct((t, LANES), F32)],
        compiler_params=_cparams(("parallel",), 48),
        name="out_proj_router",
    )(merged, w_out.astype(BF16), xp, xs, norm2_w.astype(F32).reshape(1, D_MODEL), rw, rb)


MOE_ROWS = 512
MOE_FF_TILE = 512
GATHER_UNROLL = 8


def _moe_kernel(be_ref, nb_ref, tok_hbm, h2_hbm, wrow_ref, wg_ref, wu_ref, bg_ref, bu_ref, wd_ref, bd_ref, out_ref,
                idx_smem, xbuf, xb_scr, acc_scr, idx_sem, row_sem):
    b = pl.program_id(0)
    f = pl.program_id(1)
    n_f = pl.num_programs(1)
    n_used = nb_ref[0]

    def idx_copy(block, slot):
        return pltpu.make_async_copy(tok_hbm.at[block], idx_smem.at[slot], idx_sem.at[slot])

    def issue_rows(slot):
        def body(r, carry):
            tok = idx_smem[slot, r]
            pltpu.make_async_copy(h2_hbm.at[pl.ds(tok, 1), :], xbuf.at[slot, pl.ds(r, 1), :],
                                  row_sem.at[slot]).start()
            return carry
        lax.fori_loop(0, MOE_ROWS, body, 0, unroll=GATHER_UNROLL)

    def wait_rows(slot):
        pltpu.make_async_copy(h2_hbm.at[pl.ds(0, MOE_ROWS), :], xbuf.at[slot], row_sem.at[slot]).wait()

    @pl.when(b < n_used)
    def _():
        slot = lax.rem(b, 2)
        nslot = 1 - slot

        @pl.when(f == 0)
        def _():
            @pl.when(b == 0)
            def _():
                idx_copy(0, 0).start()
                idx_copy(0, 0).wait()
                issue_rows(0)

                @pl.when(n_used > 1)
                def _():
                    idx_copy(1, 1).start()

            @pl.when(b + 1 < n_used)
            def _():
                idx_copy(b + 1, nslot).wait()
                issue_rows(nslot)

                @pl.when(b + 2 < n_used)
                def _():
                    idx_copy(b + 2, slot).start()

            wait_rows(slot)
            xb_scr[...] = xbuf[slot].astype(BF16)

        xb = xb_scr[...]
        gate = jnp.dot(xb, wg_ref[0], preferred_element_type=F32) + bg_ref[0]
        up = jnp.dot(xb, wu_ref[0], preferred_element_type=F32) + bu_ref[0]
        gate = jnp.minimum(gate, SWIGLU_LIMIT)
        up = jnp.clip(up, -SWIGLU_LIMIT, SWIGLU_LIMIT)
        act = gate * _sigmoid(SWIGLU_ALPHA * gate) * (up + 1.0)
        part = jnp.dot(act.astype(BF16), wd_ref[0], preferred_element_type=F32)

        @pl.when(f == 0)
        def _():
            acc_scr[...] = part

        @pl.when(f > 0)
        def _():
            acc_scr[...] += part

        @pl.when(f == n_f - 1)
        def _():
            out_ref[...] = (acc_scr[...] + bd_ref[0]) * wrow_ref[...]

    @pl.when(jnp.logical_and(b >= n_used, f == n_f - 1))
    def _():
        out_ref[...] = jnp.zeros_like(out_ref)


def _moe(h2, tok_sorted, w_sorted, block_expert, n_used, w_gate_up, b_gate_up, w_down, b_down):
    n_blocks = tok_sorted.shape[0]
    n_f = D_FF // MOE_FF_TILE
    wgu = w_gate_up.astype(BF16)
    wd = w_down.astype(BF16)
    bgu = b_gate_up.astype(F32).reshape(N_EXPERTS, 1, 2 * D_FF)
    bd = b_down.astype(F32).reshape(N_EXPERTS, 1, D_MODEL)

    def live(b, nb):
        return b < nb[0]

    def ff(b, f, nb):
        return jnp.where(live(b, nb), f, n_f - 1)

    def blk(b, nb):
        return jnp.minimum(b, nb[0] - 1)

    grid_spec = pltpu.PrefetchScalarGridSpec(
        num_scalar_prefetch=2,
        grid=(n_blocks, n_f),
        in_specs=[
            pl.BlockSpec(memory_space=pl.ANY),
            pl.BlockSpec(memory_space=pl.ANY),
            pl.BlockSpec((MOE_ROWS, 1), lambda b, f, be, nb: (blk(b, nb), 0)),
            pl.BlockSpec((1, D_MODEL, MOE_FF_TILE), lambda b, f, be, nb: (be[b], 0, ff(b, f, nb))),
            pl.BlockSpec((1, D_MODEL, MOE_FF_TILE), lambda b, f, be, nb: (be[b], 0, n_f + ff(b, f, nb))),
            pl.BlockSpec((1, 1, MOE_FF_TILE), lambda b, f, be, nb: (be[b], 0, ff(b, f, nb))),
            pl.BlockSpec((1, 1, MOE_FF_TILE), lambda b, f, be, nb: (be[b], 0, n_f + ff(b, f, nb))),
            pl.BlockSpec((1, MOE_FF_TILE, D_MODEL), lambda b, f, be, nb: (be[b], ff(b, f, nb), 0)),
            pl.BlockSpec((1, 1, D_MODEL), lambda b, f, be, nb: (be[b], 0, 0)),
        ],
        out_specs=pl.BlockSpec((MOE_ROWS, D_MODEL), lambda b, f, be, nb: (b, 0)),
        scratch_shapes=[
            pltpu.SMEM((2, MOE_ROWS), jnp.int32),
            pltpu.VMEM((2, MOE_ROWS, D_MODEL), F32),
            pltpu.VMEM((MOE_ROWS, D_MODEL), BF16),
            pltpu.VMEM((MOE_ROWS, D_MODEL), F32),
            pltpu.SemaphoreType.DMA((2,)),
            pltpu.SemaphoreType.DMA((2,)),
        ],
    )
    return pl.pallas_call(
        _moe_kernel,
        grid_spec=grid_spec,
        out_shape=jax.ShapeDtypeStruct((n_blocks * MOE_ROWS, D_MODEL), F32),
        compiler_params=_cparams(("arbitrary", "arbitrary"), 48),
        name="moe_experts",
    )(block_expert, n_used, tok_sorted, h2, w_sorted, wgu, wgu, bgu, bgu, wd, bd)


def _route(top_idx, top_w, n_blocks):
    t = top_idx.shape[0]
    flat_e = top_idx.reshape(-1)
    onehot = (flat_e[:, None] == jnp.arange(N_EXPERTS, dtype=jnp.int32)[None, :]).astype(jnp.int32)
    cum = jnp.cumsum(onehot, axis=0)
    rank = jnp.take_along_axis(cum, flat_e[:, None], axis=1)[:, 0] - 1
    counts = cum[-1]
    blocks_e = (counts + MOE_ROWS - 1) // MOE_ROWS
    blocks_end = jnp.cumsum(blocks_e)
    blocks_start = blocks_end - blocks_e
    dest = (blocks_start[flat_e] * MOE_ROWS + rank).astype(jnp.int32)
    n_used = blocks_end[-1].astype(jnp.int32)
    flat_tok = jnp.arange(t * TOP_K, dtype=jnp.int32) // TOP_K
    tok_sorted = jnp.zeros((n_blocks * MOE_ROWS,), jnp.int32).at[dest].set(flat_tok)
    w_sorted = jnp.zeros((n_blocks * MOE_ROWS,), F32).at[dest].set(top_w.reshape(-1))
    bidx = jnp.arange(n_blocks, dtype=jnp.int32)
    be = jnp.minimum(jnp.searchsorted(blocks_end, bidx, side='right'), N_EXPERTS - 1).astype(jnp.int32)
    be = jnp.where(bidx < n_used, be, be[jnp.maximum(n_used - 1, 0)])
    return (tok_sorted.reshape(n_blocks, MOE_ROWS), w_sorted.reshape(n_blocks * MOE_ROWS, 1), be,
            n_used.reshape(1), dest)


COMBINE_ROWS = 128


def _combine_kernel(pos_hbm, rows_hbm, x1_ref, yp_ref, ys_ref, pos_smem, gbuf, pos_sem, row_sem, *, n_prompt_tiles):
    i = pl.program_id(0)
    n = pl.num_programs(0)
    slot = lax.rem(i, 2)
    nslot = 1 - slot

    def pos_copy(tile, s):
        return pltpu.make_async_copy(pos_hbm.at[tile], pos_smem.at[s], pos_sem.at[s])

    def issue_rows(s):
        def body(r, carry):
            for kk in range(TOP_K):
                src = pos_smem[s, r * TOP_K + kk]
                pltpu.make_async_copy(rows_hbm.at[pl.ds(src, 1), :], gbuf.at[s, kk, pl.ds(r, 1), :],
                                      row_sem.at[s]).start()
            return carry
        lax.fori_loop(0, COMBINE_ROWS, body, 0, unroll=2)

    def wait_rows(s):
        for kk in range(TOP_K):
            pltpu.make_async_copy(rows_hbm.at[pl.ds(0, COMBINE_ROWS), :], gbuf.at[s, kk], row_sem.at[s]).wait()

    @pl.when(i == 0)
    def _():
        pos_copy(0, 0).start()
        pos_copy(0, 0).wait()
        issue_rows(0)

        @pl.when(n > 1)
        def _():
            pos_copy(1, 1).start()

    @pl.when(i + 1 < n)
    def _():
        pos_copy(i + 1, nslot).wait()
        issue_rows(nslot)

        @pl.when(i + 2 < n)
        def _():
            pos_copy(i + 2, slot).start()

    wait_rows(slot)
    y = x1_ref[...]
    for kk in range(TOP_K):
        y = y + gbuf[slot, kk]

    @pl.when(i < n_prompt_tiles)
    def _():
        yp_ref[...] = y

    @pl.when(i >= n_prompt_tiles)
    def _():
        ys_ref[...] = y


def _combine(x1, expert_rows, dest, tp, ts):
    t = tp + ts
    tile = _pick_tile(int(np.gcd(tp, ts)), COMBINE_ROWS)
    assert tile == COMBINE_ROWS
    npt, nst = tp // tile, ts // tile
    p_spec, s_spec = _two_group_specs(tile, npt, nst, D_MODEL)
    return pl.pallas_call(
        functools.partial(_combine_kernel, n_prompt_tiles=npt),
        grid=(npt + nst,),
        in_specs=[pl.BlockSpec(memory_space=pl.ANY), pl.BlockSpec(memory_space=pl.ANY),
                  pl.BlockSpec((tile, D_MODEL), lambda m: (m, 0))],
        out_specs=[p_spec, s_spec],
        out_shape=[jax.ShapeDtypeStruct((tp, D_MODEL), F32), jax.ShapeDtypeStruct((ts, D_MODEL), F32)],
        scratch_shapes=[
            pltpu.SMEM((2, COMBINE_ROWS * TOP_K), jnp.int32),
            pltpu.VMEM((2, TOP_K, COMBINE_ROWS, D_MODEL), F32),
            pltpu.SemaphoreType.DMA((2,)),
            pltpu.SemaphoreType.DMA((2,)),
        ],
        compiler_params=_cparams(("arbitrary",), 32),
        name="moe_combine",
    )(dest.reshape(t // tile, tile * TOP_K), expert_rows, x1)


def _sequence_flags(seq_lengths, unit):
    first, last = [], []
    for length in seq_lengths:
        n = length // unit
        first += [1] + [0] * (n - 1)
        last += [0] * (n - 1) + [1]
    return jnp.asarray(np.array([first, last], np.int32))


def _in_projections(h, w_in, q_norm_w, k_norm_w):
    w = w_in.astype(BF16)
    offs = np.cumsum([0, N_HEADS * HEAD_DIM, N_KV_HEADS * HEAD_DIM, N_KV_HEADS * HEAD_DIM, D_INNER, C_XBC,
                      2 * SSD_HEADS, D_MODEL, D_MODEL])
    seg = [w[:, offs[i]:offs[i + 1]] for i in range(8)]
    w_dt = jnp.pad(seg[5], ((0, 0), (0, LANES - 2 * SSD_HEADS)))

    def head_w_spec(tm, tn):
        return [pl.BlockSpec((1, HEAD_DIM), lambda m, j: (0, 0))]

    qw = (q_norm_w.astype(F32) * (1.0 / np.sqrt(HEAD_DIM))).reshape(1, HEAD_DIM)
    kw = k_norm_w.astype(F32).reshape(1, HEAD_DIM)
    out = {}
    out['q'] = _linear([h], [seg[0]], [qw], head_w_spec, _ep_head_norm, BF16, 1024, 512, "proj_q")
    out['k'] = _linear([h], [seg[1]], [kw], head_w_spec, _ep_head_norm, BF16, 1024, 512, "proj_k")
    out['v'] = _linear([h], [seg[2]], [], _no_aux, _ep_cast, BF16, 1024, 512, "proj_v")
    out['silu_z'] = _linear([h], [seg[3]], [], _no_aux, _ep_silu, BF16, 1024, 512, "proj_z")
    out['xbc'] = _linear([h], [seg[4]], [], _no_aux, _ep_cast, BF16, 1024, 512, "proj_xbc")
    out['dt'] = _linear([h], [w_dt], [], _no_aux, _ep_cast, F32, 1024, LANES, "proj_dt")
    out['gate_attn'] = _linear([h], [seg[6]], [], _no_aux, _ep_sigmoid, BF16, 1024, 512, "proj_gate_attn")
    out['gate_ssd'] = _linear([h], [seg[7]], [], _no_aux, _ep_sigmoid, BF16, 1024, 512, "proj_gate_ssd")
    return out


def _pre_attention(x_prompt, x_sample, p):
    xp = x_prompt.reshape(-1, D_MODEL)
    xs = x_sample.reshape(-1, D_MODEL)
    seq_lengths = [x_prompt.shape[1]] * x_prompt.shape[0] + [x_sample.shape[1]] * x_sample.shape[0]
    h = _norm1(xp, xs, p['norm1_w'][0])
    proj = _in_projections(h, p['w_in'][0], p['q_norm_w'][0], p['k_norm_w'][0])
    tq = _pick_tile(int(np.gcd.reduce(seq_lengths)), 512)
    flags = _sequence_flags(seq_lengths, tq)
    proj['attn'] = _attention(proj['q'], proj['k'], proj['v'], p['attn_sink'][0], flags, tq)
    proj['seq_lengths'] = seq_lengths
    if 'conv_w' in p:
        proj['ssd'] = _ssd(proj, seq_lengths, p['conv_w'][0], p['conv_b'][0], p['dt_bias'][0], p['a_log'][0],
                           p['d_skip'][0], p['ssd_norm_w'][0])
    return proj


def kernel(x_prompt, x_sample, norm1_w, w_in, q_norm_w, k_norm_w, attn_sink, conv_w, conv_b, dt_bias, a_log, d_skip, ssd_norm_w, w_attn_proj, w_ssd_proj, w_out, norm2_w, router_w, router_b, w_gate_up, b_gate_up, w_down, b_down):
    assert norm1_w.shape[0] == 1, "single-layer block"
    p = dict(norm1_w=norm1_w, w_in=w_in, q_norm_w=q_norm_w, k_norm_w=k_norm_w, attn_sink=attn_sink,
             conv_w=conv_w, conv_b=conv_b, dt_bias=dt_bias, a_log=a_log, d_skip=d_skip, ssd_norm_w=ssd_norm_w)
    xp = x_prompt.reshape(-1, D_MODEL)
    xs = x_sample.reshape(-1, D_MODEL)
    tp, ts = xp.shape[0], xs.shape[0]
    t = tp + ts
    pre = _pre_attention(x_prompt, x_sample, p)

    def gate_specs(tm, tn):
        return [pl.BlockSpec((tm, tn), lambda m, j: (m, j))] * 2

    merged = _linear([pre['attn'], pre['ssd']], [w_attn_proj[0].astype(BF16), w_ssd_proj[0].astype(BF16)],
                     [pre['gate_attn'], pre['gate_ssd']], gate_specs, _ep_gated_sum, BF16, 1024, 512, "branch_merge")
    x1, h2, top_idx, top_w = _out_router(merged, w_out[0], xp, xs, norm2_w[0], router_w[0], router_b[0])

    n_blocks = -(-(t * TOP_K) // MOE_ROWS) + N_EXPERTS
    tok_sorted, w_sorted, block_expert, n_used, dest = _route(top_idx[:, :TOP_K], top_w[:, :TOP_K], n_blocks)
    expert_rows = _moe(h2, tok_sorted, w_sorted, block_expert, n_used, w_gate_up[0], b_gate_up[0], w_down[0],
                       b_down[0])
    yp, ys = _combine(x1, expert_rows, dest, tp, ts)
    return yp.reshape(x_prompt.shape), ys.reshape(x_sample.shape)
```

```python
import functools

import numpy as np
import jax
import jax.numpy as jnp
from jax import lax
from jax.experimental import pallas as pl
from jax.experimental.pallas import tpu as pltpu

F32 = jnp.float32
BF16 = jnp.bfloat16
HIGHEST = lax.Precision.HIGHEST

D_MODEL = 2048
N_HEADS = 16
N_KV_HEADS = 4
Q_PER_KV = N_HEADS // N_KV_HEADS
HEAD_DIM = 128
WINDOW = 128
ATTN_BLOCK = 128
D_INNER = 2048
SSD_HEAD_DIM = 64
SSD_HEADS = D_INNER // SSD_HEAD_DIM
SSD_GROUPS = 4
HEADS_PER_GROUP = SSD_HEADS // SSD_GROUPS
GROUP_WIDTH = D_INNER // SSD_GROUPS
D_STATE = 128
CONV_K = 5
SSD_CHUNK = 128
C_XBC = D_INNER + 2 * SSD_GROUPS * D_STATE
N_EXPERTS = 32
TOP_K = 4
D_FF = D_MODEL
SWIGLU_LIMIT = 7.0
SWIGLU_ALPHA = 1.702
NORM_EPS = 1e-6
MASK_VALUE = -1e30

LANES = 128
BF16_SUBLANES = 16
MIB = 1 << 20


def _cparams(semantics, vmem_mib):
    return pltpu.CompilerParams(dimension_semantics=semantics, vmem_limit_bytes=vmem_mib * MIB)


def _sigmoid(x):
    return 1.0 / (1.0 + jnp.exp(-x))


def _pick_tile(total, preferred):
    t = min(total, preferred)
    while total % t:
        t //= 2
    return t


def _norm1_kernel(xp_ref, xs_ref, w_ref, o_ref, *, n_prompt_tiles):
    m = pl.program_id(0)

    def body(x_ref):
        x = x_ref[...]
        ms = jnp.mean(x * x, axis=-1, keepdims=True)
        o_ref[...] = (x * lax.rsqrt(ms + NORM_EPS) * w_ref[...]).astype(BF16)

    @pl.when(m < n_prompt_tiles)
    def _():
        body(xp_ref)

    @pl.when(m >= n_prompt_tiles)
    def _():
        body(xs_ref)


def _two_group_specs(tile, n_prompt_tiles, n_sample_tiles, width):
    p_spec = pl.BlockSpec((tile, width), lambda m: (jnp.minimum(m, n_prompt_tiles - 1), 0))
    s_spec = pl.BlockSpec((tile, width), lambda m: (jnp.maximum(m - n_prompt_tiles, 0), 0))
    return p_spec, s_spec


def _norm1(xp, xs, w):
    tp, ts = xp.shape[0], xs.shape[0]
    tile = _pick_tile(int(np.gcd(tp, ts)), 512)
    npt, nst = tp // tile, ts // tile
    p_spec, s_spec = _two_group_specs(tile, npt, nst, D_MODEL)
    return pl.pallas_call(
        functools.partial(_norm1_kernel, n_prompt_tiles=npt),
        grid=(npt + nst,),
        in_specs=[p_spec, s_spec, pl.BlockSpec((1, D_MODEL), lambda m: (0, 0))],
        out_specs=pl.BlockSpec((tile, D_MODEL), lambda m: (m, 0)),
        out_shape=jax.ShapeDtypeStruct((tp + ts, D_MODEL), BF16),
        compiler_params=_cparams(("parallel",), 40),
        name="norm1",
    )(xp, xs, w.reshape(1, D_MODEL))


LINEAR_TILE_M = 1024
LINEAR_TILE_N = 1024


def _linear_kernel(*refs, n_lhs, n_aux, epilogue):
    lhs = refs[:n_lhs]
    rhs = refs[n_lhs:2 * n_lhs]
    aux = refs[2 * n_lhs:2 * n_lhs + n_aux]
    out = refs[2 * n_lhs + n_aux]
    accs = [jnp.dot(l[...], r[...], preferred_element_type=F32) for l, r in zip(lhs, rhs)]
    epilogue(accs, aux, out)


def _ep_cast(accs, aux, out):
    out[...] = accs[0].astype(out.dtype)


def _ep_silu(accs, aux, out):
    a = accs[0]
    out[...] = (a * _sigmoid(a)).astype(out.dtype)


def _ep_sigmoid(accs, aux, out):
    out[...] = _sigmoid(accs[0]).astype(out.dtype)


def _ep_head_norm(accs, aux, out):
    a = accs[0]
    w = aux[0][...]
    for j in range(a.shape[1] // HEAD_DIM):
        s = a[:, j * HEAD_DIM:(j + 1) * HEAD_DIM]
        ms = jnp.mean(s * s, axis=-1, keepdims=True)
        out[:, j * HEAD_DIM:(j + 1) * HEAD_DIM] = (s * lax.rsqrt(ms + NORM_EPS) * w).astype(out.dtype)


def _ep_gated_sum(accs, aux, out):
    out[...] = (aux[0][...].astype(F32) * accs[0] + aux[1][...].astype(F32) * accs[1]).astype(out.dtype)


def _linear(lhs_list, rhs_list, aux_list, aux_specs, epilogue, out_dtype, tm, tn, name):
    t, k = lhs_list[0].shape
    n = rhs_list[0].shape[1]
    tm = _pick_tile(t, tm)
    tn = _pick_tile(n, tn)
    in_specs = ([pl.BlockSpec((tm, k), lambda m, j: (m, 0)) for _ in lhs_list]
                + [pl.BlockSpec((k, tn), lambda m, j: (0, j)) for _ in rhs_list]
                + list(aux_specs(tm, tn)))
    return pl.pallas_call(
        functools.partial(_linear_kernel, n_lhs=len(lhs_list), n_aux=len(aux_list), epilogue=epilogue),
        grid=(t // tm, n // tn),
        in_specs=in_specs,
        out_specs=pl.BlockSpec((tm, tn), lambda m, j: (m, j)),
        out_shape=jax.ShapeDtypeStruct((t, n), out_dtype),
        compiler_params=_cparams(("parallel", "arbitrary"), 56),
        name=name,
    )(*lhs_list, *rhs_list, *aux_list)


def _no_aux(tm, tn):
    return []


def _attn_kernel(flags_ref, slope_ref, sink_ref, q_ref, k_ref, kp_ref, kn_ref, v_ref, vp_ref, vn_ref, o_ref,
                 *, n_sub):
    i = pl.program_id(0)
    g = pl.program_id(1)
    has_prev = flags_ref[0, i] == 0
    has_next = flags_ref[1, i] == 0

    qi = lax.broadcasted_iota(jnp.int32, (ATTN_BLOCK, 3 * ATTN_BLOCK), 0)
    kj = lax.broadcasted_iota(jnp.int32, (ATTN_BLOCK, 3 * ATTN_BLOCK), 1)
    dist = jnp.abs(ATTN_BLOCK + qi - kj)
    in_window = dist <= WINDOW
    dist_f = dist.astype(F32)
    is_prev_blk = kj < ATTN_BLOCK
    is_next_blk = kj >= 2 * ATTN_BLOCK

    for j in range(n_sub):
        rows = slice(j * ATTN_BLOCK, (j + 1) * ATTN_BLOCK)
        prev_rows = slice((j - 1) * ATTN_BLOCK, j * ATTN_BLOCK)
        next_rows = slice((j + 1) * ATTN_BLOCK, (j + 2) * ATTN_BLOCK)
        k_prev = kp_ref[...] if j == 0 else k_ref[prev_rows, :]
        v_prev = vp_ref[...] if j == 0 else v_ref[prev_rows, :]
        k_next = kn_ref[...] if j == n_sub - 1 else k_ref[next_rows, :]
        v_next = vn_ref[...] if j == n_sub - 1 else v_ref[next_rows, :]
        k_band = jnp.concatenate([k_prev, k_ref[rows, :], k_next], axis=0)
        v_band = jnp.concatenate([v_prev, v_ref[rows, :], v_next], axis=0)
        valid = in_window
        if j == 0:
            valid = valid & (has_prev | jnp.logical_not(is_prev_blk))
        if j == n_sub - 1:
            valid = valid & (has_next | jnp.logical_not(is_next_blk))
        q_stack = jnp.concatenate(
            [q_ref[rows, r * HEAD_DIM:(r + 1) * HEAD_DIM] for r in range(Q_PER_KV)], axis=0)
        s_all = lax.dot_general(q_stack, k_band, (((1,), (1,)), ((), ())), preferred_element_type=F32)
        for r in range(Q_PER_KV):
            head = g * Q_PER_KV + r
            s = s_all[r * ATTN_BLOCK:(r + 1) * ATTN_BLOCK, :] - slope_ref[head] * dist_f
            s = jnp.where(valid, s, MASK_VALUE)
            sink = sink_ref[head]
            m = jnp.maximum(jnp.max(s, axis=-1, keepdims=True), sink)
            p = jnp.exp(s - m)
            denom = jnp.sum(p, axis=-1, keepdims=True) + jnp.exp(sink - m)
            o = jnp.dot(p.astype(BF16), v_band, preferred_element_type=F32)
            o_ref[rows, r * HEAD_DIM:(r + 1) * HEAD_DIM] = (o / denom).astype(o_ref.dtype)


def _attention(q, k, v, sink, chunk_flags, tq):
    t = q.shape[0]
    n_chunks = t // tq
    n_blocks = t // ATTN_BLOCK
    sub = tq // ATTN_BLOCK
    slopes = jnp.asarray(2.0 ** (-8.0 * (np.arange(N_HEADS, dtype=np.float32) + 1.0) / N_HEADS), F32)
    gw = Q_PER_KV * HEAD_DIM

    def own(width):
        return pl.BlockSpec((tq, width), lambda i, g, *_: (i, g))

    prev = pl.BlockSpec((ATTN_BLOCK, HEAD_DIM), lambda i, g, *_: (jnp.maximum(i * sub - 1, 0), g))
    nxt = pl.BlockSpec((ATTN_BLOCK, HEAD_DIM), lambda i, g, *_: (jnp.minimum((i + 1) * sub, n_blocks - 1), g))
    grid_spec = pltpu.PrefetchScalarGridSpec(
        num_scalar_prefetch=3,
        grid=(n_chunks, N_KV_HEADS),
        in_specs=[own(gw), own(HEAD_DIM), prev, nxt, own(HEAD_DIM), prev, nxt],
        out_specs=own(gw),
    )
    return pl.pallas_call(
        functools.partial(_attn_kernel, n_sub=sub),
        grid_spec=grid_spec,
        out_shape=jax.ShapeDtypeStruct((t, N_HEADS * HEAD_DIM), BF16),
        compiler_params=_cparams(("parallel", "arbitrary"), 32),
        name="banded_attention",
    )(chunk_flags, slopes, sink.astype(F32), q, k, k, k, v, v, v)


CONV_HALO = BF16_SUBLANES
CONV_PAD = CONV_K // 2
CONV_TAP_ROWS = 8
CONV_COL_TILE = 512
HEAD_PAIR_WIDTH = 2 * SSD_HEAD_DIM


def _ssd_kernel(flags_ref, xbc_ref, xprev_ref, xnext_ref, dt_ref, convw_ref, convb_ref, dtb_ref, alog_ref, *rest,
                reverse):
    if reverse:
        yf_ref, sz_ref, dskip_ref, normw_ref, out_ref, ext_scr, xc_scr, y_scr, state_scr = rest
    else:
        out_ref, ext_scr, xc_scr, y_scr, state_scr = rest
    i = pl.program_id(0)
    c = pl.num_programs(0) - 1 - i if reverse else i
    seq_first = flags_ref[0, c] == 1
    seq_last = flags_ref[1, c] == 1
    L = SSD_CHUNK

    ext_scr[0:CONV_HALO, :] = jnp.where(seq_first, 0.0, xprev_ref[...].astype(F32))
    ext_scr[CONV_HALO:CONV_HALO + L, :] = xbc_ref[...].astype(F32)
    ext_scr[CONV_HALO + L:, :] = jnp.where(seq_last, 0.0, xnext_ref[...].astype(F32))
    for ct in range(C_XBC // CONV_COL_TILE):
        cols = slice(ct * CONV_COL_TILE, (ct + 1) * CONV_COL_TILE)
        acc = jnp.broadcast_to(convb_ref[:, cols], (L, CONV_COL_TILE))
        for j in range(CONV_K):
            r0 = CONV_HALO - CONV_PAD + j
            acc = acc + ext_scr[r0:r0 + L, cols] * convw_ref[j:j + 1, cols]
        xc_scr[:, cols] = acc * _sigmoid(acc)

    col = lax.broadcasted_iota(jnp.int32, (1, LANES), 1)
    a_neg = jnp.where(col < 2 * SSD_HEADS, -jnp.exp(alog_ref[...]), 0.0)
    xdt = dt_ref[...] + dtb_ref[...]
    dt = jnp.maximum(xdt, 0.0) + jnp.log1p(jnp.exp(-jnp.abs(xdt)))
    a = dt * a_neg
    ri = lax.broadcasted_iota(jnp.int32, (L, L), 0)
    ci = lax.broadcasted_iota(jnp.int32, (L, L), 1)
    causal = (ri <= ci) if reverse else (ri >= ci)
    a_cum = jnp.dot(causal.astype(F32), a, precision=HIGHEST, preferred_element_type=F32)
    a_cum_t = a_cum.T
    dt_t = dt.T
    edge = 0 if reverse else L - 1
    a_total = a_cum[edge:edge + 1, :]
    w_state = dt * jnp.exp(a_total - a_cum)
    dir_off = SSD_HEADS if reverse else 0
    hr = lax.broadcasted_iota(jnp.int32, (LANES, D_INNER), 0)
    hc = lax.broadcasted_iota(jnp.int32, (LANES, D_INNER), 1)
    expand = (hr == dir_off + lax.shift_right_logical(hc, int(np.log2(SSD_HEAD_DIM)))).astype(F32)
    chunk_decay = jnp.dot(jnp.broadcast_to(jnp.exp(a_total), (8, LANES)), expand, precision=HIGHEST,
                          preferred_element_type=F32)[0:1, :]

    @pl.when(seq_last if reverse else seq_first)
    def _():
        state_scr[...] = jnp.zeros_like(state_scr)

    lo = lax.broadcasted_iota(jnp.int32, (L, LANES), 1) < SSD_HEAD_DIM
    for g in range(SSD_GROUPS):
        b_g = xc_scr[:, D_INNER + g * D_STATE:D_INNER + (g + 1) * D_STATE]
        c_g = xc_scr[:, D_INNER + (SSD_GROUPS + g) * D_STATE:D_INNER + (SSD_GROUPS + g + 1) * D_STATE]
        c_bf = c_g.astype(BF16)
        cb = lax.dot_general(c_bf, b_g.astype(BF16), (((1,), (1,)), ((), ())), preferred_element_type=F32)
        b_t = b_g.T.astype(BF16)
        state = state_scr[g]
        y_off = jnp.dot(c_bf, state.astype(BF16), preferred_element_type=F32)
        xw_parts = []
        for p in range(HEADS_PER_GROUP // 2):
            gcols = slice(g * GROUP_WIDTH + p * HEAD_PAIR_WIDTH, g * GROUP_WIDTH + (p + 1) * HEAD_PAIR_WIDTH)
            m_parts, e_parts, w_parts = [], [], []
            for hh in range(2):
                k = dir_off + g * HEADS_PER_GROUP + 2 * p + hh
                colb = jnp.broadcast_to(a_cum[:, k:k + 1], (L, L))
                rowb = jnp.broadcast_to(a_cum_t[k:k + 1, :], (L, L))
                dtrow = jnp.broadcast_to(dt_t[k:k + 1, :], (L, L))
                decay = jnp.exp(jnp.where(causal, colb - rowb, -jnp.inf))
                m_parts.append((cb * decay * dtrow).astype(BF16))
                e_parts.append(jnp.exp(colb))
                w_parts.append(jnp.broadcast_to(w_state[:, k:k + 1], (L, LANES)))
            x_pair = xc_scr[:, gcols]
            rhs = jnp.concatenate([jnp.where(lo, x_pair, 0.0), jnp.where(lo, 0.0, x_pair)], axis=0).astype(BF16)
            y = jnp.dot(jnp.concatenate(m_parts, axis=1), rhs, preferred_element_type=F32)
            y = y + y_off[:, p * HEAD_PAIR_WIDTH:(p + 1) * HEAD_PAIR_WIDTH] * jnp.where(lo, e_parts[0], e_parts[1])
            y_scr[:, gcols] = y
            xw_parts.append((x_pair * jnp.where(lo, w_parts[0], w_parts[1])).astype(BF16))
        xw = jnp.concatenate(xw_parts, axis=1)
        state_scr[g] = (state * chunk_decay[:, g * GROUP_WIDTH:(g + 1) * GROUP_WIDTH]
                        + jnp.dot(b_t, xw, preferred_element_type=F32))

    if not reverse:
        out_ref[...] = y_scr[...].astype(out_ref.dtype)
    else:
        for g in range(SSD_GROUPS):
            cols = slice(g * GROUP_WIDTH, (g + 1) * GROUP_WIDTH)
            y = y_scr[:, cols] + yf_ref[:, cols].astype(F32) + xc_scr[:, cols] * dskip_ref[:, cols]
            yg = y * sz_ref[:, cols].astype(F32)
            ms = jnp.mean(yg * yg, axis=-1, keepdims=True)
            out_ref[:, cols] = (yg * lax.rsqrt(ms + NORM_EPS) * normw_ref[:, cols]).astype(out_ref.dtype)


def _ssd_pass(chunk_flags, xbc, dt, conv_w, conv_b, dt_bias, a_log, reverse, extras=()):
    t = xbc.shape[0]
    n_chunks = t // SSD_CHUNK
    halo_per_chunk = SSD_CHUNK // CONV_HALO
    n_halo_blocks = t // CONV_HALO

    def cidx(i):
        return n_chunks - 1 - i if reverse else i

    def row(width, rows=SSD_CHUNK):
        return pl.BlockSpec((rows, width), lambda i, *_: (cidx(i), 0))

    def const(shape):
        return pl.BlockSpec(shape, lambda i, *_: (0, 0))

    prev = pl.BlockSpec((CONV_HALO, C_XBC), lambda i, *_: (jnp.maximum(cidx(i) * halo_per_chunk - 1, 0), 0))
    nxt = pl.BlockSpec((CONV_HALO, C_XBC),
                       lambda i, *_: (jnp.minimum((cidx(i) + 1) * halo_per_chunk, n_halo_blocks - 1), 0))
    in_specs = [row(C_XBC), prev, nxt, row(LANES), const((CONV_TAP_ROWS, C_XBC)), const((1, C_XBC)),
                const((1, LANES)), const((1, LANES))]
    if reverse:
        in_specs += [row(D_INNER), row(D_INNER), const((1, D_INNER)), const((1, D_INNER))]
    grid_spec = pltpu.PrefetchScalarGridSpec(
        num_scalar_prefetch=1,
        grid=(n_chunks,),
        in_specs=in_specs,
        out_specs=row(D_INNER),
        scratch_shapes=[
            pltpu.VMEM((SSD_CHUNK + 2 * CONV_HALO, C_XBC), F32),
            pltpu.VMEM((SSD_CHUNK, C_XBC), F32),
            pltpu.VMEM((SSD_CHUNK, D_INNER), F32),
            pltpu.VMEM((SSD_GROUPS, D_STATE, GROUP_WIDTH), F32),
        ],
    )
    pad = LANES - 2 * SSD_HEADS
    return pl.pallas_call(
        functools.partial(_ssd_kernel, reverse=reverse),
        grid_spec=grid_spec,
        out_shape=jax.ShapeDtypeStruct((t, D_INNER), BF16),
        compiler_params=_cparams(("arbitrary",), 40),
        name="ssd_bwd" if reverse else "ssd_fwd",
    )(chunk_flags, xbc, xbc, xbc, dt,
      jnp.pad(conv_w.astype(F32), ((0, CONV_TAP_ROWS - CONV_K), (0, 0))), conv_b.astype(F32).reshape(1, C_XBC),
      jnp.pad(dt_bias.astype(F32).reshape(1, -1), ((0, 0), (0, pad))),
      jnp.pad(a_log.astype(F32).reshape(1, -1), ((0, 0), (0, pad))), *extras)


def _ssd(proj, seq_lengths, conv_w, conv_b, dt_bias, a_log, d_skip, ssd_norm_w):
    flags = _sequence_flags(seq_lengths, SSD_CHUNK)
    args = (flags, proj['xbc'], proj['dt'], conv_w, conv_b, dt_bias, a_log)
    y_fwd = _ssd_pass(*args, reverse=False)
    d_lanes = jnp.repeat(d_skip.astype(F32), SSD_HEAD_DIM).reshape(1, D_INNER)
    return _ssd_pass(*args, reverse=True,
                     extras=(y_fwd, proj['silu_z'], d_lanes, ssd_norm_w.astype(F32).reshape(1, D_INNER)))


def _out_router_kernel(m_ref, w_ref, xp_ref, xs_ref, nw_ref, rw_ref, rb_ref, x1_ref, h2_ref, ti_ref, tw_ref,
                       *, n_prompt_tiles):
    i = pl.program_id(0)
    acc = jnp.dot(m_ref[...], w_ref[...], preferred_element_type=F32)

    def finish(x_ref):
        x1 = x_ref[...] + acc
        x1_ref[...] = x1
        ms = jnp.mean(x1 * x1, axis=-1, keepdims=True)
        h2 = x1 * lax.rsqrt(ms + NORM_EPS) * nw_ref[...]
        h2_ref[...] = h2
        v = jnp.dot(h2.astype(BF16), rw_ref[...], preferred_element_type=F32) + rb_ref[...]
        lane = lax.broadcasted_iota(jnp.int32, v.shape, 1)
        vals, idxs = [], []
        for _ in range(TOP_K):
            top = jnp.max(v, axis=-1, keepdims=True)
            idx = jnp.min(jnp.where(v == top, lane, LANES), axis=-1, keepdims=True)
            vals.append(top)
            idxs.append(idx)
            v = jnp.where(lane == idx, -jnp.inf, v)
        es = [jnp.exp(val - vals[0]) for val in vals]
        total = es[0]
        for e in es[1:]:
            total = total + e
        ti = jnp.zeros(v.shape, jnp.int32)
        tw = jnp.zeros(v.shape, F32)
        for kk in range(TOP_K):
            ti = jnp.where(lane == kk, idxs[kk], ti)
            tw = jnp.where(lane == kk, es[kk] / total, tw)
        ti_ref[...] = ti
        tw_ref[...] = tw

    @pl.when(i < n_prompt_tiles)
    def _():
        finish(xp_ref)

    @pl.when(i >= n_prompt_tiles)
    def _():
        finish(xs_ref)


def _out_router(merged, w_out, xp, xs, norm2_w, router_w, router_b):
    tp, ts = xp.shape[0], xs.shape[0]
    t = tp + ts
    tile = _pick_tile(int(np.gcd(tp, ts)), 256)
    npt, nst = tp // tile, ts // tile
    p_spec, s_spec = _two_group_specs(tile, npt, nst, D_MODEL)
    rw = jnp.pad(router_w.astype(BF16), ((0, 0), (0, LANES - N_EXPERTS)))
    rb = jnp.pad(router_b.astype(F32).reshape(1, N_EXPERTS), ((0, 0), (0, LANES - N_EXPERTS)),
                 constant_values=-jnp.inf)

    def const(shape):
        return pl.BlockSpec(shape, lambda m: (0, 0))

    def row(width):
        return pl.BlockSpec((tile, width), lambda m: (m, 0))

    return pl.pallas_call(
        functools.partial(_out_router_kernel, n_prompt_tiles=npt),
        grid=(npt + nst,),
        in_specs=[row(D_MODEL), const((D_MODEL, D_MODEL)), p_spec, s_spec, const((1, D_MODEL)),
                  const((D_MODEL, LANES)), const((1, LANES))],
        out_specs=[row(D_MODEL), row(D_MODEL), row(LANES), row(LANES)],
        out_shape=[jax.ShapeDtypeStruct((t, D_MODEL), F32), jax.ShapeDtypeStruct((t, D_MODEL), F32),
                   jax.ShapeDtypeStruct((t, LANES), jnp.int32), jax.ShapeDtypeStruct((t, LANES), F32)],
        compiler_params=_cparams(("parallel",), 48),
        name="out_proj_router",
    )(merged, w_out.astype(BF16), xp, xs, norm2_w.astype(F32).reshape(1, D_MODEL), rw, rb)


MOE_ROWS = 512
MOE_FF_TILE = 1024
MOE_GU_GROUP = 256
GATHER_UNROLL = 8


def _moe_kernel(be_ref, nb_ref, tok_hbm, h2_hbm, wgu_ref, bgu_ref, wd_ref, bd_ref, out_ref,
                idx_smem, xbuf, xb_scr, idx_sem, row_sem):
    b = pl.program_id(0)
    f = pl.program_id(1)
    n_f = pl.num_programs(1)
    n_used = nb_ref[0]

    def idx_copy(block, slot):
        return pltpu.make_async_copy(tok_hbm.at[block], idx_smem.at[slot], idx_sem.at[slot])

    def issue_rows(slot):
        def body(r, carry):
            tok = idx_smem[slot, r]
            pltpu.make_async_copy(h2_hbm.at[pl.ds(tok, 1), :], xbuf.at[slot, pl.ds(r, 1), :],
                                  row_sem.at[slot]).start()
            return carry
        lax.fori_loop(0, MOE_ROWS, body, 0, unroll=GATHER_UNROLL)

    def wait_rows(slot):
        pltpu.make_async_copy(h2_hbm.at[pl.ds(0, MOE_ROWS), :], xbuf.at[slot], row_sem.at[slot]).wait()

    @pl.when(b < n_used)
    def _():
        slot = lax.rem(b, 2)
        nslot = 1 - slot

        @pl.when(f == 0)
        def _():
            @pl.when(b == 0)
            def _():
                idx_copy(0, 0).start()
                idx_copy(0, 0).wait()
                issue_rows(0)

                @pl.when(n_used > 1)
                def _():
                    idx_copy(1, 1).start()

            @pl.when(b + 1 < n_used)
            def _():
                idx_copy(b + 1, nslot).wait()
                issue_rows(nslot)

                @pl.when(b + 2 < n_used)
                def _():
                    idx_copy(b + 2, slot).start()

            wait_rows(slot)
            xb_scr[...] = xbuf[slot].astype(BF16)

        gu = jnp.dot(xb_scr[...], wgu_ref[0], preferred_element_type=F32) + bgu_ref[0]
        acts = []
        for j in range(MOE_FF_TILE // MOE_GU_GROUP):
            gate = gu[:, 2 * j * MOE_GU_GROUP:(2 * j + 1) * MOE_GU_GROUP]
            up = gu[:, (2 * j + 1) * MOE_GU_GROUP:(2 * j + 2) * MOE_GU_GROUP]
            gate = jnp.minimum(gate, SWIGLU_LIMIT)
            up = jnp.clip(up, -SWIGLU_LIMIT, SWIGLU_LIMIT)
            acts.append((gate * _sigmoid(SWIGLU_ALPHA * gate) * (up + 1.0)).astype(BF16))
        act = jnp.concatenate(acts, axis=1)
        base = jnp.where(f == 0, jnp.broadcast_to(bd_ref[0], out_ref.shape), out_ref[...])
        out_ref[...] = base + jnp.dot(act, wd_ref[0], preferred_element_type=F32)

    @pl.when(jnp.logical_and(b >= n_used, f == n_f - 1))
    def _():
        out_ref[...] = jnp.zeros_like(out_ref)


def _pair_gate_up(w):
    lead = w.shape[:-1]
    w = w.reshape(*lead, 2, D_FF // MOE_GU_GROUP, MOE_GU_GROUP)
    return jnp.swapaxes(w, -3, -2).reshape(*lead, 2 * D_FF)


def _moe(h2, tok_sorted, block_expert, n_used, w_gate_up, b_gate_up, w_down, b_down):
    n_blocks = tok_sorted.shape[0]
    n_f = D_FF // MOE_FF_TILE
    wgu = _pair_gate_up(w_gate_up.astype(BF16))
    wd = w_down.astype(BF16)
    bgu = _pair_gate_up(b_gate_up.astype(F32)).reshape(N_EXPERTS, 1, 2 * D_FF)
    bd = b_down.astype(F32).reshape(N_EXPERTS, 1, D_MODEL)

    def live(b, nb):
        return b < nb[0]

    def ff(b, f, nb):
        return jnp.where(live(b, nb), f, n_f - 1)

    grid_spec = pltpu.PrefetchScalarGridSpec(
        num_scalar_prefetch=2,
        grid=(n_blocks, n_f),
        in_specs=[
            pl.BlockSpec(memory_space=pl.ANY),
            pl.BlockSpec(memory_space=pl.ANY),
            pl.BlockSpec((1, D_MODEL, 2 * MOE_FF_TILE), lambda b, f, be, nb: (be[b], 0, ff(b, f, nb))),
            pl.BlockSpec((1, 1, 2 * MOE_FF_TILE), lambda b, f, be, nb: (be[b], 0, ff(b, f, nb))),
            pl.BlockSpec((1, MOE_FF_TILE, D_MODEL), lambda b, f, be, nb: (be[b], ff(b, f, nb), 0)),
            pl.BlockSpec((1, 1, D_MODEL), lambda b, f, be, nb: (be[b], 0, 0)),
        ],
        out_specs=pl.BlockSpec((MOE_ROWS, D_MODEL), lambda b, f, be, nb: (b, 0)),
        scratch_shapes=[
            pltpu.SMEM((2, MOE_ROWS), jnp.int32),
            pltpu.VMEM((2, MOE_ROWS, D_MODEL), F32),
            pltpu.VMEM((MOE_ROWS, D_MODEL), BF16),
            pltpu.SemaphoreType.DMA((2,)),
            pltpu.SemaphoreType.DMA((2,)),
        ],
    )
    return pl.pallas_call(
        _moe_kernel,
        grid_spec=grid_spec,
        out_shape=jax.ShapeDtypeStruct((n_blocks * MOE_ROWS, D_MODEL), F32),
        compiler_params=_cparams(("arbitrary", "arbitrary"), 56),
        name="moe_experts",
    )(block_expert, n_used, tok_sorted, h2, wgu, bgu, wd, bd)


def _route(top_idx, n_blocks):
    t = top_idx.shape[0]
    flat_e = top_idx.reshape(-1)
    onehot = (flat_e[:, None] == jnp.arange(N_EXPERTS, dtype=jnp.int32)[None, :]).astype(jnp.int32)
    cum = jnp.cumsum(onehot, axis=0)
    rank = jnp.take_along_axis(cum, flat_e[:, None], axis=1)[:, 0] - 1
    counts = cum[-1]
    blocks_e = (counts + MOE_ROWS - 1) // MOE_ROWS
    blocks_end = jnp.cumsum(blocks_e)
    blocks_start = blocks_end - blocks_e
    dest = (blocks_start[flat_e] * MOE_ROWS + rank).astype(jnp.int32)
    n_used = blocks_end[-1].astype(jnp.int32)
    flat_tok = jnp.arange(t * TOP_K, dtype=jnp.int32) // TOP_K
    tok_sorted = jnp.zeros((n_blocks * MOE_ROWS,), jnp.int32).at[dest].set(flat_tok, unique_indices=True)
    bidx = jnp.arange(n_blocks, dtype=jnp.int32)
    be = jnp.minimum(jnp.searchsorted(blocks_end, bidx, side='right'), N_EXPERTS - 1).astype(jnp.int32)
    be = jnp.where(bidx < n_used, be, be[jnp.maximum(n_used - 1, 0)])
    return tok_sorted.reshape(n_blocks, MOE_ROWS), be, n_used.reshape(1), dest


COMBINE_ROWS = 128


def _combine_kernel(pos_hbm, rows_hbm, x1_ref, tw_ref, yp_ref, ys_ref, pos_smem, gbuf, pos_sem, row_sem,
                    *, n_prompt_tiles):
    i = pl.program_id(0)
    n = pl.num_programs(0)
    slot = lax.rem(i, 2)
    nslot = 1 - slot

    def pos_copy(tile, s):
        return pltpu.make_async_copy(pos_hbm.at[tile], pos_smem.at[s], pos_sem.at[s])

    def issue_rows(s):
        def body(r, carry):
            for kk in range(TOP_K):
                src = pos_smem[s, r * TOP_K + kk]
                pltpu.make_async_copy(rows_hbm.at[pl.ds(src, 1), :], gbuf.at[s, kk, pl.ds(r, 1), :],
                                      row_sem.at[s]).start()
            return carry
        lax.fori_loop(0, COMBINE_ROWS, body, 0, unroll=2)

    def wait_rows(s):
        for kk in range(TOP_K):
            pltpu.make_async_copy(rows_hbm.at[pl.ds(0, COMBINE_ROWS), :], gbuf.at[s, kk], row_sem.at[s]).wait()

    @pl.when(i == 0)
    def _():
        pos_copy(0, 0).start()
        pos_copy(0, 0).wait()
        issue_rows(0)

        @pl.when(n > 1)
        def _():
            pos_copy(1, 1).start()

    @pl.when(i + 1 < n)
    def _():
        pos_copy(i + 1, nslot).wait()
        issue_rows(nslot)

        @pl.when(i + 2 < n)
        def _():
            pos_copy(i + 2, slot).start()

    wait_rows(slot)
    y = x1_ref[...]
    tw = tw_ref[...]
    for kk in range(TOP_K):
        y = y + tw[:, kk:kk + 1] * gbuf[slot, kk]

    @pl.when(i < n_prompt_tiles)
    def _():
        yp_ref[...] = y

    @pl.when(i >= n_prompt_tiles)
    def _():
        ys_ref[...] = y


def _combine(x1, expert_rows, dest, top_w, tp, ts):
    t = tp + ts
    tile = _pick_tile(int(np.gcd(tp, ts)), COMBINE_ROWS)
    assert tile == COMBINE_ROWS
    npt, nst = tp // tile, ts // tile
    p_spec, s_spec = _two_group_specs(tile, npt, nst, D_MODEL)
    return pl.pallas_call(
        functools.partial(_combine_kernel, n_prompt_tiles=npt),
        grid=(npt + nst,),
        in_specs=[pl.BlockSpec(memory_space=pl.ANY), pl.BlockSpec(memory_space=pl.ANY),
                  pl.BlockSpec((tile, D_MODEL), lambda m: (m, 0)), pl.BlockSpec((tile, LANES), lambda m: (m, 0))],
        out_specs=[p_spec, s_spec],
        out_shape=[jax.ShapeDtypeStruct((tp, D_MODEL), F32), jax.ShapeDtypeStruct((ts, D_MODEL), F32)],
        scratch_shapes=[
            pltpu.SMEM((2, COMBINE_ROWS * TOP_K), jnp.int32),
            pltpu.VMEM((2, TOP_K, COMBINE_ROWS, D_MODEL), F32),
            pltpu.SemaphoreType.DMA((2,)),
            pltpu.SemaphoreType.DMA((2,)),
        ],
        compiler_params=_cparams(("arbitrary",), 32),
        name="moe_combine",
    )(dest.reshape(t // tile, tile * TOP_K), expert_rows, x1, top_w)


def _sequence_flags(seq_lengths, unit):
    first, last = [], []
    for length in seq_lengths:
        n = length // unit
        first += [1] + [0] * (n - 1)
        last += [0] * (n - 1) + [1]
    return jnp.asarray(np.array([first, last], np.int32))


def _in_projections(h, w_in, q_norm_w, k_norm_w):
    w = w_in.astype(BF16)
    offs = np.cumsum([0, N_HEADS * HEAD_DIM, N_KV_HEADS * HEAD_DIM, N_KV_HEADS * HEAD_DIM, D_INNER, C_XBC,
                      2 * SSD_HEADS, D_MODEL, D_MODEL])
    seg = [w[:, offs[i]:offs[i + 1]] for i in range(8)]
    w_dt = jnp.pad(seg[5], ((0, 0), (0, LANES - 2 * SSD_HEADS)))

    def head_w_spec(tm, tn):
        return [pl.BlockSpec((1, HEAD_DIM), lambda m, j: (0, 0))]

    qw = (q_norm_w.astype(F32) * (1.0 / np.sqrt(HEAD_DIM))).reshape(1, HEAD_DIM)
    kw = k_norm_w.astype(F32).reshape(1, HEAD_DIM)
    out = {}
    tm, tn = LINEAR_TILE_M, LINEAR_TILE_N
    out['q'] = _linear([h], [seg[0]], [qw], head_w_spec, _ep_head_norm, BF16, tm, tn, "proj_q")
    out['k'] = _linear([h], [seg[1]], [kw], head_w_spec, _ep_head_norm, BF16, tm, tn, "proj_k")
    out['v'] = _linear([h], [seg[2]], [], _no_aux, _ep_cast, BF16, tm, tn, "proj_v")
    out['silu_z'] = _linear([h], [seg[3]], [], _no_aux, _ep_silu, BF16, tm, tn, "proj_z")
    out['xbc'] = _linear([h], [seg[4]], [], _no_aux, _ep_cast, BF16, tm, tn, "proj_xbc")
    out['dt'] = _linear([h], [w_dt], [], _no_aux, _ep_cast, F32, tm, tn, "proj_dt")
    out['gate_attn'] = _linear([h], [seg[6]], [], _no_aux, _ep_sigmoid, BF16, tm, tn, "proj_gate_attn")
    out['gate_ssd'] = _linear([h], [seg[7]], [], _no_aux, _ep_sigmoid, BF16, tm, tn, "proj_gate_ssd")
    return out


def _pre_attention(x_prompt, x_sample, p):
    xp = x_prompt.reshape(-1, D_MODEL)
    xs = x_sample.reshape(-1, D_MODEL)
    seq_lengths = [x_prompt.shape[1]] * x_prompt.shape[0] + [x_sample.shape[1]] * x_sample.shape[0]
    h = _norm1(xp, xs, p['norm1_w'][0])
    proj = _in_projections(h, p['w_in'][0], p['q_norm_w'][0], p['k_norm_w'][0])
    tq = _pick_tile(int(np.gcd.reduce(seq_lengths)), 512)
    flags = _sequence_flags(seq_lengths, tq)
    proj['attn'] = _attention(proj['q'], proj['k'], proj['v'], p['attn_sink'][0], flags, tq)
    proj['seq_lengths'] = seq_lengths
    if 'conv_w' in p:
        proj['ssd'] = _ssd(proj, seq_lengths, p['conv_w'][0], p['conv_b'][0], p['dt_bias'][0], p['a_log'][0],
                           p['d_skip'][0], p['ssd_norm_w'][0])
    return proj


def kernel(x_prompt, x_sample, norm1_w, w_in, q_norm_w, k_norm_w, attn_sink, conv_w, conv_b, dt_bias, a_log, d_skip, ssd_norm_w, w_attn_proj, w_ssd_proj, w_out, norm2_w, router_w, router_b, w_gate_up, b_gate_up, w_down, b_down):
    assert norm1_w.shape[0] == 1, "single-layer block"
    p = dict(norm1_w=norm1_w, w_in=w_in, q_norm_w=q_norm_w, k_norm_w=k_norm_w, attn_sink=attn_sink,
             conv_w=conv_w, conv_b=conv_b, dt_bias=dt_bias, a_log=a_log, d_skip=d_skip, ssd_norm_w=ssd_norm_w)
    xp = x_prompt.reshape(-1, D_MODEL)
    xs = x_sample.reshape(-1, D_MODEL)
    tp, ts = xp.shape[0], xs.shape[0]
    t = tp + ts
    pre = _pre_attention(x_prompt, x_sample, p)

    def gate_specs(tm, tn):
        return [pl.BlockSpec((tm, tn), lambda m, j: (m, j))] * 2

    merged = _linear([pre['attn'], pre['ssd']], [w_attn_proj[0].astype(BF16), w_ssd_proj[0].astype(BF16)],
                     [pre['gate_attn'], pre['gate_ssd']], gate_specs, _ep_gated_sum, BF16, LINEAR_TILE_M,
                     LINEAR_TILE_N, "branch_merge")
    x1, h2, top_idx, top_w = _out_router(merged, w_out[0], xp, xs, norm2_w[0], router_w[0], router_b[0])

    n_blocks = -(-(t * TOP_K) // MOE_ROWS) + N_EXPERTS
    tok_sorted, block_expert, n_used, dest = _route(top_idx[:, :TOP_K], n_blocks)
    expert_rows = _moe(h2, tok_sorted, block_expert, n_used, w_gate_up[0], b_gate_up[0], w_down[0], b_down[0])
    yp, ys = _combine(x1, expert_rows, dest, top_w, tp, ts)
    return yp.reshape(x_prompt.shape), ys.reshape(x_sample.shape)
```

```python
import functools

import numpy as np
import jax
import jax.numpy as jnp
from jax import lax
from jax.experimental import pallas as pl
from jax.experimental.pallas import tpu as pltpu

F32 = jnp.float32
BF16 = jnp.bfloat16
HIGHEST = lax.Precision.HIGHEST

D_MODEL = 2048
N_HEADS = 16
N_KV_HEADS = 4
Q_PER_KV = N_HEADS // N_KV_HEADS
HEAD_DIM = 128
WINDOW = 128
ATTN_BLOCK = 128
D_INNER = 2048
SSD_HEAD_DIM = 64
SSD_HEADS = D_INNER // SSD_HEAD_DIM
SSD_GROUPS = 4
HEADS_PER_GROUP = SSD_HEADS // SSD_GROUPS
GROUP_WIDTH = D_INNER // SSD_GROUPS
D_STATE = 128
CONV_K = 5
SSD_CHUNK = 128
C_XBC = D_INNER + 2 * SSD_GROUPS * D_STATE
N_EXPERTS = 32
TOP_K = 4
D_FF = D_MODEL
SWIGLU_LIMIT = 7.0
SWIGLU_ALPHA = 1.702
NORM_EPS = 1e-6
MASK_VALUE = -1e30

LANES = 128
BF16_SUBLANES = 16
MIB = 1 << 20


def _cparams(semantics, vmem_mib):
    return pltpu.CompilerParams(dimension_semantics=semantics, vmem_limit_bytes=vmem_mib * MIB)


def _sigmoid(x):
    return 1.0 / (1.0 + jnp.exp(-x))


def _pick_tile(total, preferred):
    t = min(total, preferred)
    while total % t:
        t //= 2
    return t


def _norm1_kernel(xp_ref, xs_ref, w_ref, o_ref, *, n_prompt_tiles):
    m = pl.program_id(0)

    def body(x_ref):
        x = x_ref[...]
        ms = jnp.mean(x * x, axis=-1, keepdims=True)
        o_ref[...] = (x * lax.rsqrt(ms + NORM_EPS) * w_ref[...]).astype(BF16)

    @pl.when(m < n_prompt_tiles)
    def _():
        body(xp_ref)

    @pl.when(m >= n_prompt_tiles)
    def _():
        body(xs_ref)


def _two_group_specs(tile, n_prompt_tiles, n_sample_tiles, width):
    p_spec = pl.BlockSpec((tile, width), lambda m: (jnp.minimum(m, n_prompt_tiles - 1), 0))
    s_spec = pl.BlockSpec((tile, width), lambda m: (jnp.maximum(m - n_prompt_tiles, 0), 0))
    return p_spec, s_spec


def _norm1(xp, xs, w):
    tp, ts = xp.shape[0], xs.shape[0]
    tile = _pick_tile(int(np.gcd(tp, ts)), 512)
    npt, nst = tp // tile, ts // tile
    p_spec, s_spec = _two_group_specs(tile, npt, nst, D_MODEL)
    return pl.pallas_call(
        functools.partial(_norm1_kernel, n_prompt_tiles=npt),
        grid=(npt + nst,),
        in_specs=[p_spec, s_spec, pl.BlockSpec((1, D_MODEL), lambda m: (0, 0))],
        out_specs=pl.BlockSpec((tile, D_MODEL), lambda m: (m, 0)),
        out_shape=jax.ShapeDtypeStruct((tp + ts, D_MODEL), BF16),
        compiler_params=_cparams(("parallel",), 40),
        name="norm1",
    )(xp, xs, w.reshape(1, D_MODEL))


LINEAR_TILE_M = 1024
LINEAR_TILE_N = 1024


def _linear_kernel(*refs, n_lhs, n_aux, epilogue):
    lhs = refs[:n_lhs]
    rhs = refs[n_lhs:2 * n_lhs]
    aux = refs[2 * n_lhs:2 * n_lhs + n_aux]
    out = refs[2 * n_lhs + n_aux]
    accs = [jnp.dot(l[...], r[...], preferred_element_type=F32) for l, r in zip(lhs, rhs)]
    epilogue(accs, aux, out)


def _ep_cast(accs, aux, out):
    out[...] = accs[0].astype(out.dtype)


def _ep_silu(accs, aux, out):
    a = accs[0]
    out[...] = (a * _sigmoid(a)).astype(out.dtype)


def _ep_sigmoid(accs, aux, out):
    out[...] = _sigmoid(accs[0]).astype(out.dtype)


def _ep_head_norm(accs, aux, out):
    a = accs[0]
    w = aux[0][...]
    for j in range(a.shape[1] // HEAD_DIM):
        s = a[:, j * HEAD_DIM:(j + 1) * HEAD_DIM]
        ms = jnp.mean(s * s, axis=-1, keepdims=True)
        out[:, j * HEAD_DIM:(j + 1) * HEAD_DIM] = (s * lax.rsqrt(ms + NORM_EPS) * w).astype(out.dtype)


def _ep_gated_sum(accs, aux, out):
    out[...] = (aux[0][...].astype(F32) * accs[0] + aux[1][...].astype(F32) * accs[1]).astype(out.dtype)


def _linear(lhs_list, rhs_list, aux_list, aux_specs, epilogue, out_dtype, tm, tn, name):
    t, k = lhs_list[0].shape
    n = rhs_list[0].shape[1]
    tm = _pick_tile(t, tm)
    tn = _pick_tile(n, tn)
    in_specs = ([pl.BlockSpec((tm, k), lambda m, j: (m, 0)) for _ in lhs_list]
                + [pl.BlockSpec((k, tn), lambda m, j: (0, j)) for _ in rhs_list]
                + list(aux_specs(tm, tn)))
    return pl.pallas_call(
        functools.partial(_linear_kernel, n_lhs=len(lhs_list), n_aux=len(aux_list), epilogue=epilogue),
        grid=(t // tm, n // tn),
        in_specs=in_specs,
        out_specs=pl.BlockSpec((tm, tn), lambda m, j: (m, j)),
        out_shape=jax.ShapeDtypeStruct((t, n), out_dtype),
        compiler_params=_cparams(("parallel", "arbitrary"), 56),
        name=name,
    )(*lhs_list, *rhs_list, *aux_list)


def _no_aux(tm, tn):
    return []


def _attn_kernel(flags_ref, slope_ref, sink_ref, q_ref, k_ref, kp_ref, kn_ref, v_ref, vp_ref, vn_ref, o_ref,
                 *, n_sub):
    i = pl.program_id(0)
    g = pl.program_id(1)
    has_prev = flags_ref[0, i] == 0
    has_next = flags_ref[1, i] == 0

    qi = lax.broadcasted_iota(jnp.int32, (ATTN_BLOCK, 3 * ATTN_BLOCK), 0)
    kj = lax.broadcasted_iota(jnp.int32, (ATTN_BLOCK, 3 * ATTN_BLOCK), 1)
    dist = jnp.abs(ATTN_BLOCK + qi - kj)
    in_window = dist <= WINDOW
    dist_f = dist.astype(F32)
    is_prev_blk = kj < ATTN_BLOCK
    is_next_blk = kj >= 2 * ATTN_BLOCK

    for j in range(n_sub):
        rows = slice(j * ATTN_BLOCK, (j + 1) * ATTN_BLOCK)
        prev_rows = slice((j - 1) * ATTN_BLOCK, j * ATTN_BLOCK)
        next_rows = slice((j + 1) * ATTN_BLOCK, (j + 2) * ATTN_BLOCK)
        k_prev = kp_ref[...] if j == 0 else k_ref[prev_rows, :]
        v_prev = vp_ref[...] if j == 0 else v_ref[prev_rows, :]
        k_next = kn_ref[...] if j == n_sub - 1 else k_ref[next_rows, :]
        v_next = vn_ref[...] if j == n_sub - 1 else v_ref[next_rows, :]
        k_band = jnp.concatenate([k_prev, k_ref[rows, :], k_next], axis=0)
        v_band = jnp.concatenate([v_prev, v_ref[rows, :], v_next], axis=0)
        valid = in_window
        if j == 0:
            valid = valid & (has_prev | jnp.logical_not(is_prev_blk))
        if j == n_sub - 1:
            valid = valid & (has_next | jnp.logical_not(is_next_blk))
        q_stack = jnp.concatenate(
            [q_ref[rows, r * HEAD_DIM:(r + 1) * HEAD_DIM] for r in range(Q_PER_KV)], axis=0)
        s_all = lax.dot_general(q_stack, k_band, (((1,), (1,)), ((), ())), preferred_element_type=F32)
        for r in range(Q_PER_KV):
            head = g * Q_PER_KV + r
            s = s_all[r * ATTN_BLOCK:(r + 1) * ATTN_BLOCK, :] - slope_ref[head] * dist_f
            s = jnp.where(valid, s, MASK_VALUE)
            sink = sink_ref[head]
            m = jnp.maximum(jnp.max(s, axis=-1, keepdims=True), sink)
            p = jnp.exp(s - m)
            denom = jnp.sum(p, axis=-1, keepdims=True) + jnp.exp(sink - m)
            o = jnp.dot(p.astype(BF16), v_band, preferred_element_type=F32)
            o_ref[rows, r * HEAD_DIM:(r + 1) * HEAD_DIM] = (o / denom).astype(o_ref.dtype)


def _attention(q, k, v, sink, chunk_flags, tq):
    t = q.shape[0]
    n_chunks = t // tq
    n_blocks = t // ATTN_BLOCK
    sub = tq // ATTN_BLOCK
    slopes = jnp.asarray(2.0 ** (-8.0 * (np.arange(N_HEADS, dtype=np.float32) + 1.0) / N_HEADS), F32)
    gw = Q_PER_KV * HEAD_DIM

    def own(width):
        return pl.BlockSpec((tq, width), lambda i, g, *_: (i, g))

    prev = pl.BlockSpec((ATTN_BLOCK, HEAD_DIM), lambda i, g, *_: (jnp.maximum(i * sub - 1, 0), g))
    nxt = pl.BlockSpec((ATTN_BLOCK, HEAD_DIM), lambda i, g, *_: (jnp.minimum((i + 1) * sub, n_blocks - 1), g))
    grid_spec = pltpu.PrefetchScalarGridSpec(
        num_scalar_prefetch=3,
        grid=(n_chunks, N_KV_HEADS),
        in_specs=[own(gw), own(HEAD_DIM), prev, nxt, own(HEAD_DIM), prev, nxt],
        out_specs=own(gw),
    )
    return pl.pallas_call(
        functools.partial(_attn_kernel, n_sub=sub),
        grid_spec=grid_spec,
        out_shape=jax.ShapeDtypeStruct((t, N_HEADS * HEAD_DIM), BF16),
        compiler_params=_cparams(("parallel", "arbitrary"), 32),
        name="banded_attention",
    )(chunk_flags, slopes, sink.astype(F32), q, k, k, k, v, v, v)


CONV_HALO = BF16_SUBLANES
CONV_PAD = CONV_K // 2
CONV_TAP_ROWS = 8
CONV_COL_TILE = 512
HEAD_PAIR_WIDTH = 2 * SSD_HEAD_DIM


def _ssd_kernel(flags_ref, *rest, reverse):
    i = pl.program_id(0)
    c = pl.num_programs(0) - 1 - i if reverse else i
    seq_first = flags_ref[0, c] == 1
    seq_last = flags_ref[1, c] == 1
    L = SSD_CHUNK

    if reverse:
        xc_ref, dt_ref, dtb_ref, alog_ref, yf_ref, sz_ref, dskip_ref, normw_ref, out_ref, y_scr, state_scr = rest

        def xc(cols):
            return xc_ref[:, cols].astype(F32)
    else:
        (xbc_ref, xprev_ref, xnext_ref, dt_ref, convw_ref, convb_ref, dtb_ref, alog_ref, out_ref, xc_out_ref,
         ext_scr, xc_scr, y_scr, state_scr) = rest

        def xc(cols):
            return xc_scr[:, cols]

        zero_halo = jnp.zeros((CONV_HALO, C_XBC), BF16)
        ext_scr[0:CONV_HALO, :] = jnp.where(seq_first, zero_halo, xprev_ref[...])
        ext_scr[CONV_HALO:CONV_HALO + L, :] = xbc_ref[...]
        ext_scr[CONV_HALO + L:, :] = jnp.where(seq_last, zero_halo, xnext_ref[...])
        out_row = lax.broadcasted_iota(jnp.int32, (L, L + 2 * CONV_HALO), 0)
        src_row = lax.broadcasted_iota(jnp.int32, (L, L + 2 * CONV_HALO), 1)
        for ct in range(C_XBC // CONV_COL_TILE):
            cols = slice(ct * CONV_COL_TILE, (ct + 1) * CONV_COL_TILE)
            ext = ext_scr[:, cols]
            acc = convb_ref[:, cols] + xbc_ref[:, cols].astype(F32) * convw_ref[CONV_PAD:CONV_PAD + 1, cols]
            for j in range(CONV_K):
                if j == CONV_PAD:
                    continue
                shift = (src_row == out_row + (CONV_HALO - CONV_PAD + j)).astype(BF16)
                acc = acc + jnp.dot(shift, ext, preferred_element_type=F32) * convw_ref[j:j + 1, cols]
            conv = acc * _sigmoid(acc)
            xc_scr[:, cols] = conv
            xc_out_ref[:, cols] = conv.astype(xc_out_ref.dtype)

    col = lax.broadcasted_iota(jnp.int32, (1, LANES), 1)
    a_neg = jnp.where(col < 2 * SSD_HEADS, -jnp.exp(alog_ref[...]), 0.0)
    xdt = dt_ref[...] + dtb_ref[...]
    dt = jnp.maximum(xdt, 0.0) + jnp.log1p(jnp.exp(-jnp.abs(xdt)))
    a = dt * a_neg
    ri = lax.broadcasted_iota(jnp.int32, (L, L), 0)
    ci = lax.broadcasted_iota(jnp.int32, (L, L), 1)
    causal = (ri <= ci) if reverse else (ri >= ci)
    a_cum = jnp.dot(causal.astype(F32), a, precision=HIGHEST, preferred_element_type=F32)
    a_cum_t = a_cum.T
    dt_t = dt.T
    edge = 0 if reverse else L - 1
    a_total = a_cum[edge:edge + 1, :]
    w_state = dt * jnp.exp(a_total - a_cum)
    dir_off = SSD_HEADS if reverse else 0
    hr = lax.broadcasted_iota(jnp.int32, (LANES, D_INNER), 0)
    hc = lax.broadcasted_iota(jnp.int32, (LANES, D_INNER), 1)
    expand = (hr == dir_off + lax.shift_right_logical(hc, int(np.log2(SSD_HEAD_DIM)))).astype(F32)
    chunk_decay = jnp.dot(jnp.broadcast_to(jnp.exp(a_total), (8, LANES)), expand, precision=HIGHEST,
                          preferred_element_type=F32)[0:1, :]

    @pl.when(seq_last if reverse else seq_first)
    def _():
        state_scr[...] = jnp.zeros_like(state_scr)

    lo = lax.broadcasted_iota(jnp.int32, (L, LANES), 1) < SSD_HEAD_DIM
    for g in range(SSD_GROUPS):
        b_g = xc(slice(D_INNER + g * D_STATE, D_INNER + (g + 1) * D_STATE))
        c_g = xc(slice(D_INNER + (SSD_GROUPS + g) * D_STATE, D_INNER + (SSD_GROUPS + g + 1) * D_STATE))
        c_bf = c_g.astype(BF16)
        cb = lax.dot_general(c_bf, b_g.astype(BF16), (((1,), (1,)), ((), ())), preferred_element_type=F32)
        b_t = b_g.T.astype(BF16)
        state = state_scr[g]
        y_off = jnp.dot(c_bf, state.astype(BF16), preferred_element_type=F32)
        xw_parts = []
        for p in range(HEADS_PER_GROUP // 2):
            gcols = slice(g * GROUP_WIDTH + p * HEAD_PAIR_WIDTH, g * GROUP_WIDTH + (p + 1) * HEAD_PAIR_WIDTH)
            m_parts, e_parts, w_parts = [], [], []
            for hh in range(2):
                k = dir_off + g * HEADS_PER_GROUP + 2 * p + hh
                colb = jnp.broadcast_to(a_cum[:, k:k + 1], (L, L))
                rowb = jnp.broadcast_to(a_cum_t[k:k + 1, :], (L, L))
                dtrow = jnp.broadcast_to(dt_t[k:k + 1, :], (L, L))
                decay = jnp.exp(jnp.where(causal, colb - rowb, -jnp.inf))
                m_parts.append((cb * decay * dtrow).astype(BF16))
                e_parts.append(jnp.exp(colb))
                w_parts.append(jnp.broadcast_to(w_state[:, k:k + 1], (L, LANES)))
            x_pair = xc(gcols)
            rhs = jnp.concatenate([jnp.where(lo, x_pair, 0.0), jnp.where(lo, 0.0, x_pair)], axis=0).astype(BF16)
            y = jnp.dot(jnp.concatenate(m_parts, axis=1), rhs, preferred_element_type=F32)
            y = y + y_off[:, p * HEAD_PAIR_WIDTH:(p + 1) * HEAD_PAIR_WIDTH] * jnp.where(lo, e_parts[0], e_parts[1])
            y_scr[:, gcols] = y
            xw_parts.append((x_pair * jnp.where(lo, w_parts[0], w_parts[1])).astype(BF16))
        xw = jnp.concatenate(xw_parts, axis=1)
        state_scr[g] = (state * chunk_decay[:, g * GROUP_WIDTH:(g + 1) * GROUP_WIDTH]
                        + jnp.dot(b_t, xw, preferred_element_type=F32))

    if not reverse:
        out_ref[...] = y_scr[...].astype(out_ref.dtype)
    else:
        for g in range(SSD_GROUPS):
            cols = slice(g * GROUP_WIDTH, (g + 1) * GROUP_WIDTH)
            y = y_scr[:, cols] + yf_ref[:, cols].astype(F32) + xc(cols) * dskip_ref[:, cols]
            yg = y * sz_ref[:, cols].astype(F32)
            ms = jnp.mean(yg * yg, axis=-1, keepdims=True)
            out_ref[:, cols] = (yg * lax.rsqrt(ms + NORM_EPS) * normw_ref[:, cols]).astype(out_ref.dtype)


def _ssd(proj, seq_lengths, conv_w, conv_b, dt_bias, a_log, d_skip, ssd_norm_w):
    xbc, dt = proj['xbc'], proj['dt']
    t = xbc.shape[0]
    n_chunks = t // SSD_CHUNK
    halo_per_chunk = SSD_CHUNK // CONV_HALO
    n_halo_blocks = t // CONV_HALO
    flags = _sequence_flags(seq_lengths, SSD_CHUNK)
    pad = LANES - 2 * SSD_HEADS
    dtb = jnp.pad(dt_bias.astype(F32).reshape(1, -1), ((0, 0), (0, pad)))
    alog = jnp.pad(a_log.astype(F32).reshape(1, -1), ((0, 0), (0, pad)))
    scan_scratch = [pltpu.VMEM((SSD_CHUNK, D_INNER), F32), pltpu.VMEM((SSD_GROUPS, D_STATE, GROUP_WIDTH), F32)]

    def const(shape):
        return pl.BlockSpec(shape, lambda i, *_: (0, 0))

    def row(width):
        return pl.BlockSpec((SSD_CHUNK, width), lambda i, *_: (i, 0))

    def rev_row(width):
        return pl.BlockSpec((SSD_CHUNK, width), lambda i, *_: (n_chunks - 1 - i, 0))

    prev = pl.BlockSpec((CONV_HALO, C_XBC), lambda i, *_: (jnp.maximum(i * halo_per_chunk - 1, 0), 0))
    nxt = pl.BlockSpec((CONV_HALO, C_XBC),
                       lambda i, *_: (jnp.minimum((i + 1) * halo_per_chunk, n_halo_blocks - 1), 0))
    y_fwd, xc = pl.pallas_call(
        functools.partial(_ssd_kernel, reverse=False),
        grid_spec=pltpu.PrefetchScalarGridSpec(
            num_scalar_prefetch=1,
            grid=(n_chunks,),
            in_specs=[row(C_XBC), prev, nxt, row(LANES), const((CONV_TAP_ROWS, C_XBC)), const((1, C_XBC)),
                      const((1, LANES)), const((1, LANES))],
            out_specs=[row(D_INNER), row(C_XBC)],
            scratch_shapes=[pltpu.VMEM((SSD_CHUNK + 2 * CONV_HALO, C_XBC), BF16),
                            pltpu.VMEM((SSD_CHUNK, C_XBC), F32)] + scan_scratch,
        ),
        out_shape=[jax.ShapeDtypeStruct((t, D_INNER), BF16), jax.ShapeDtypeStruct((t, C_XBC), BF16)],
        compiler_params=_cparams(("arbitrary",), 40),
        name="ssd_fwd",
    )(flags, xbc, xbc, xbc, dt, jnp.pad(conv_w.astype(F32), ((0, CONV_TAP_ROWS - CONV_K), (0, 0))),
      conv_b.astype(F32).reshape(1, C_XBC), dtb, alog)

    d_lanes = jnp.repeat(d_skip.astype(F32), SSD_HEAD_DIM).reshape(1, D_INNER)
    return pl.pallas_call(
        functools.partial(_ssd_kernel, reverse=True),
        grid_spec=pltpu.PrefetchScalarGridSpec(
            num_scalar_prefetch=1,
            grid=(n_chunks,),
            in_specs=[rev_row(C_XBC), rev_row(LANES), const((1, LANES)), const((1, LANES)), rev_row(D_INNER),
                      rev_row(D_INNER), const((1, D_INNER)), const((1, D_INNER))],
            out_specs=rev_row(D_INNER),
            scratch_shapes=scan_scratch,
        ),
        out_shape=jax.ShapeDtypeStruct((t, D_INNER), BF16),
        compiler_params=_cparams(("arbitrary",), 40),
        name="ssd_bwd",
    )(flags, xc, dt, dtb, alog, y_fwd, proj['silu_z'], d_lanes, ssd_norm_w.astype(F32).reshape(1, D_INNER))


RANK_LANE = TOP_K


def _out_router_kernel(m_ref, w_ref, xp_ref, xs_ref, nw_ref, rw_ref, rb_ref, x1_ref, h2_ref, ti_ref, tw_ref,
                       cnt_ref, *, n_prompt_tiles):
    i = pl.program_id(0)
    acc = jnp.dot(m_ref[...], w_ref[...], preferred_element_type=F32)

    @pl.when(i == 0)
    def _():
        cnt_ref[...] = jnp.zeros_like(cnt_ref)

    def finish(x):
        x1 = x + acc
        x1_ref[...] = x1
        ms = jnp.mean(x1 * x1, axis=-1, keepdims=True)
        h2 = x1 * lax.rsqrt(ms + NORM_EPS) * nw_ref[...]
        h2_ref[...] = h2
        v = jnp.dot(h2.astype(BF16), rw_ref[...], preferred_element_type=F32) + rb_ref[...]
        lane = lax.broadcasted_iota(jnp.int32, v.shape, 1)
        vals, idxs = [], []
        for _ in range(TOP_K):
            top = jnp.max(v, axis=-1, keepdims=True)
            idx = jnp.min(jnp.where(v == top, lane, LANES), axis=-1, keepdims=True)
            vals.append(top)
            idxs.append(idx)
            v = jnp.where(lane == idx, -jnp.inf, v)
        es = [jnp.exp(val - vals[0]) for val in vals]
        total = es[0]
        for e in es[1:]:
            total = total + e
        n_rows = v.shape[0]
        chosen = jnp.zeros(v.shape, F32)
        for kk in range(TOP_K):
            chosen = chosen + (lane == idxs[kk]).astype(F32)
        ri = lax.broadcasted_iota(jnp.int32, (n_rows, n_rows), 0)
        ci = lax.broadcasted_iota(jnp.int32, (n_rows, n_rows), 1)
        before = jnp.dot((ci < ri).astype(BF16), chosen.astype(BF16), preferred_element_type=F32)
        before = before + cnt_ref[0:1, :]
        cnt_ref[...] = cnt_ref[...] + jnp.sum(chosen, axis=0, keepdims=True)
        ti = jnp.zeros(v.shape, jnp.int32)
        tw = jnp.zeros(v.shape, F32)
        for kk in range(TOP_K):
            rank = jnp.sum(jnp.where(lane == idxs[kk], before, 0.0), axis=-1, keepdims=True)
            ti = jnp.where(lane == kk, idxs[kk], ti)
            ti = jnp.where(lane == RANK_LANE + kk, rank.astype(jnp.int32), ti)
            tw = jnp.where(lane == kk, es[kk] / total, tw)
        ti_ref[...] = ti
        tw_ref[...] = tw

    finish(jnp.where(i < n_prompt_tiles, xp_ref[...], xs_ref[...]))


def _out_router(merged, w_out, xp, xs, norm2_w, router_w, router_b):
    tp, ts = xp.shape[0], xs.shape[0]
    t = tp + ts
    tile = _pick_tile(int(np.gcd(tp, ts)), 256)
    npt, nst = tp // tile, ts // tile
    p_spec, s_spec = _two_group_specs(tile, npt, nst, D_MODEL)
    rw = jnp.pad(router_w.astype(BF16), ((0, 0), (0, LANES - N_EXPERTS)))
    rb = jnp.pad(router_b.astype(F32).reshape(1, N_EXPERTS), ((0, 0), (0, LANES - N_EXPERTS)),
                 constant_values=-jnp.inf)

    def const(shape):
        return pl.BlockSpec(shape, lambda m: (0, 0))

    def row(width):
        return pl.BlockSpec((tile, width), lambda m: (m, 0))

    return pl.pallas_call(
        functools.partial(_out_router_kernel, n_prompt_tiles=npt),
        grid=(npt + nst,),
        in_specs=[row(D_MODEL), const((D_MODEL, D_MODEL)), p_spec, s_spec, const((1, D_MODEL)),
                  const((D_MODEL, LANES)), const((1, LANES))],
        out_specs=[row(D_MODEL), row(D_MODEL), row(LANES), row(LANES), const((8, LANES))],
        out_shape=[jax.ShapeDtypeStruct((t, D_MODEL), F32), jax.ShapeDtypeStruct((t, D_MODEL), F32),
                   jax.ShapeDtypeStruct((t, LANES), jnp.int32), jax.ShapeDtypeStruct((t, LANES), F32),
                   jax.ShapeDtypeStruct((8, LANES), F32)],
        compiler_params=_cparams(("arbitrary",), 48),
        name="out_proj_router",
    )(merged, w_out.astype(BF16), xp, xs, norm2_w.astype(F32).reshape(1, D_MODEL), rw, rb)


MOE_ROWS = 512
MOE_FF_TILE = 1024
GATHER_UNROLL = 8


def _moe_kernel(be_ref, nb_ref, tok_hbm, h2_hbm, wg_ref, wu_ref, bg_ref, bu_ref, wd_ref, bd_ref, out_ref,
                idx_smem, xbuf, xb_scr, idx_sem, row_sem):
    b = pl.program_id(0)
    f = pl.program_id(1)
    n_f = pl.num_programs(1)
    n_used = nb_ref[0]

    def idx_copy(block, slot):
        return pltpu.make_async_copy(tok_hbm.at[block], idx_smem.at[slot], idx_sem.at[slot])

    def issue_rows(slot):
        def body(r, carry):
            tok = idx_smem[slot, r]
            pltpu.make_async_copy(h2_hbm.at[pl.ds(tok, 1), :], xbuf.at[slot, pl.ds(r, 1), :],
                                  row_sem.at[slot]).start()
            return carry
        lax.fori_loop(0, MOE_ROWS, body, 0, unroll=GATHER_UNROLL)

    def wait_rows(slot):
        pltpu.make_async_copy(h2_hbm.at[pl.ds(0, MOE_ROWS), :], xbuf.at[slot], row_sem.at[slot]).wait()

    @pl.when(b < n_used)
    def _():
        slot = lax.rem(b, 2)
        nslot = 1 - slot

        @pl.when(f == 0)
        def _():
            @pl.when(b == 0)
            def _():
                idx_copy(0, 0).start()
                idx_copy(0, 0).wait()
                issue_rows(0)

                @pl.when(n_used > 1)
                def _():
                    idx_copy(1, 1).start()

            @pl.when(b + 1 < n_used)
            def _():
                idx_copy(b + 1, nslot).wait()
                for s in range(2):
                    @pl.when(nslot == s)
                    def _():
                        issue_rows(s)

                @pl.when(b + 2 < n_used)
                def _():
                    idx_copy(b + 2, slot).start()

            wait_rows(slot)
            xb_scr[...] = xbuf[slot].astype(BF16)

        xb = xb_scr[...]
        gate = jnp.dot(xb, wg_ref[0], preferred_element_type=F32) + bg_ref[0]
        up = jnp.dot(xb, wu_ref[0], preferred_element_type=F32) + bu_ref[0]
        gate = jnp.minimum(gate, SWIGLU_LIMIT)
        up = jnp.clip(up, -SWIGLU_LIMIT, SWIGLU_LIMIT)
        act = (gate * _sigmoid(SWIGLU_ALPHA * gate) * (up + 1.0)).astype(BF16)
        base = jnp.where(f == 0, jnp.broadcast_to(bd_ref[0], out_ref.shape), out_ref[...])
        out_ref[...] = base + jnp.dot(act, wd_ref[0], preferred_element_type=F32)

    @pl.when(jnp.logical_and(b >= n_used, f == n_f - 1))
    def _():
        out_ref[...] = jnp.zeros_like(out_ref)


def _moe(h2, tok_sorted, block_expert, n_used, w_gate_up, b_gate_up, w_down, b_down):
    n_blocks = tok_sorted.shape[0]
    n_f = D_FF // MOE_FF_TILE
    wgu = w_gate_up.astype(BF16)
    wd = w_down.astype(BF16)
    bgu = b_gate_up.astype(F32).reshape(N_EXPERTS, 1, 2 * D_FF)
    bd = b_down.astype(F32).reshape(N_EXPERTS, 1, D_MODEL)

    def live(b, nb):
        return b < nb[0]

    def ff(b, f, nb):
        return jnp.where(live(b, nb), f, n_f - 1)

    grid_spec = pltpu.PrefetchScalarGridSpec(
        num_scalar_prefetch=2,
        grid=(n_blocks, n_f),
        in_specs=[
            pl.BlockSpec(memory_space=pl.ANY),
            pl.BlockSpec(memory_space=pl.ANY),
            pl.BlockSpec((1, D_MODEL, MOE_FF_TILE), lambda b, f, be, nb: (be[b], 0, ff(b, f, nb))),
            pl.BlockSpec((1, D_MODEL, MOE_FF_TILE), lambda b, f, be, nb: (be[b], 0, n_f + ff(b, f, nb))),
            pl.BlockSpec((1, 1, MOE_FF_TILE), lambda b, f, be, nb: (be[b], 0, ff(b, f, nb))),
            pl.BlockSpec((1, 1, MOE_FF_TILE), lambda b, f, be, nb: (be[b], 0, n_f + ff(b, f, nb))),
            pl.BlockSpec((1, MOE_FF_TILE, D_MODEL), lambda b, f, be, nb: (be[b], ff(b, f, nb), 0)),
            pl.BlockSpec((1, 1, D_MODEL), lambda b, f, be, nb: (be[b], 0, 0)),
        ],
        out_specs=pl.BlockSpec((MOE_ROWS, D_MODEL), lambda b, f, be, nb: (b, 0)),
        scratch_shapes=[
            pltpu.SMEM((2, MOE_ROWS), jnp.int32),
            pltpu.VMEM((2, MOE_ROWS, D_MODEL), F32),
            pltpu.VMEM((MOE_ROWS, D_MODEL), BF16),
            pltpu.SemaphoreType.DMA((2,)),
            pltpu.SemaphoreType.DMA((2,)),
        ],
    )
    return pl.pallas_call(
        _moe_kernel,
        grid_spec=grid_spec,
        out_shape=jax.ShapeDtypeStruct((n_blocks * MOE_ROWS, D_MODEL), F32),
        compiler_params=_cparams(("arbitrary", "arbitrary"), 56),
        name="moe_experts",
    )(block_expert, n_used, tok_sorted, h2, wgu, wgu, bgu, bgu, wd, bd)


def _route(top_idx, rank, counts, n_blocks):
    t = top_idx.shape[0]
    flat_e = top_idx.reshape(-1)
    rank = rank.reshape(-1)
    blocks_e = (counts + MOE_ROWS - 1) // MOE_ROWS
    blocks_end = jnp.cumsum(blocks_e)
    blocks_start = blocks_end - blocks_e
    dest = (blocks_start[flat_e] * MOE_ROWS + rank).astype(jnp.int32)
    n_used = blocks_end[-1].astype(jnp.int32)
    flat_tok = jnp.arange(t * TOP_K, dtype=jnp.int32) // TOP_K
    tok_sorted = jnp.zeros((n_blocks * MOE_ROWS,), jnp.int32).at[dest].set(flat_tok, unique_indices=True)
    bidx = jnp.arange(n_blocks, dtype=jnp.int32)
    be = jnp.minimum(jnp.sum((blocks_end[None, :] <= bidx[:, None]).astype(jnp.int32), axis=1), N_EXPERTS - 1)
    be = jnp.where(bidx < n_used, be, be[jnp.maximum(n_used - 1, 0)])
    return tok_sorted.reshape(n_blocks, MOE_ROWS), be, n_used.reshape(1), dest


COMBINE_ROWS = 128


def _combine_kernel(pos_hbm, rows_hbm, x1_ref, tw_ref, yp_ref, ys_ref, pos_smem, gbuf, pos_sem, row_sem,
                    *, n_prompt_tiles):
    i = pl.program_id(0)
    n = pl.num_programs(0)
    slot = lax.rem(i, 2)
    nslot = 1 - slot

    def pos_copy(tile, s):
        return pltpu.make_async_copy(pos_hbm.at[tile], pos_smem.at[s], pos_sem.at[s])

    def issue_rows(s):
        def body(r, carry):
            for kk in range(TOP_K):
                src = pos_smem[s, r * TOP_K + kk]
                pltpu.make_async_copy(rows_hbm.at[pl.ds(src, 1), :], gbuf.at[s, kk, pl.ds(r, 1), :],
                                      row_sem.at[s]).start()
            return carry
        lax.fori_loop(0, COMBINE_ROWS, body, 0, unroll=2)

    def wait_rows(s):
        for kk in range(TOP_K):
            pltpu.make_async_copy(rows_hbm.at[pl.ds(0, COMBINE_ROWS), :], gbuf.at[s, kk], row_sem.at[s]).wait()

    @pl.when(i == 0)
    def _():
        pos_copy(0, 0).start()
        pos_copy(0, 0).wait()
        issue_rows(0)

        @pl.when(n > 1)
        def _():
            pos_copy(1, 1).start()

    @pl.when(i + 1 < n)
    def _():
        pos_copy(i + 1, nslot).wait()
        for s in range(2):
            @pl.when(nslot == s)
            def _():
                issue_rows(s)

        @pl.when(i + 2 < n)
        def _():
            pos_copy(i + 2, slot).start()

    wait_rows(slot)
    tw = tw_ref[...]
    y = x1_ref[...]
    for kk in range(TOP_K):
        y = y + tw[:, kk:kk + 1] * gbuf[slot, kk]

    @pl.when(i < n_prompt_tiles)
    def _():
        yp_ref[...] = y

    @pl.when(i >= n_prompt_tiles)
    def _():
        ys_ref[...] = y


def _combine(x1, expert_rows, dest, top_w, tp, ts):
    t = tp + ts
    tile = _pick_tile(int(np.gcd(tp, ts)), COMBINE_ROWS)
    assert tile == COMBINE_ROWS
    npt, nst = tp // tile, ts // tile
    p_spec, s_spec = _two_group_specs(tile, npt, nst, D_MODEL)
    return pl.pallas_call(
        functools.partial(_combine_kernel, n_prompt_tiles=npt),
        grid=(npt + nst,),
        in_specs=[pl.BlockSpec(memory_space=pl.ANY), pl.BlockSpec(memory_space=pl.ANY),
                  pl.BlockSpec((tile, D_MODEL), lambda m: (m, 0)), pl.BlockSpec((tile, LANES), lambda m: (m, 0))],
        out_specs=[p_spec, s_spec],
        out_shape=[jax.ShapeDtypeStruct((tp, D_MODEL), F32), jax.ShapeDtypeStruct((ts, D_MODEL), F32)],
        scratch_shapes=[
            pltpu.SMEM((2, COMBINE_ROWS * TOP_K), jnp.int32),
            pltpu.VMEM((2, TOP_K, COMBINE_ROWS, D_MODEL), F32),
            pltpu.SemaphoreType.DMA((2,)),
            pltpu.SemaphoreType.DMA((2,)),
        ],
        compiler_params=_cparams(("arbitrary",), 32),
        name="moe_combine",
    )(dest.reshape(t // tile, tile * TOP_K), expert_rows, x1, top_w)


def _sequence_flags(seq_lengths, unit):
    first, last = [], []
    for length in seq_lengths:
        n = length // unit
        first += [1] + [0] * (n - 1)
        last += [0] * (n - 1) + [1]
    return jnp.asarray(np.array([first, last], np.int32))


def _in_projections(h, w_in, q_norm_w, k_norm_w):
    w = w_in.astype(BF16)
    offs = np.cumsum([0, N_HEADS * HEAD_DIM, N_KV_HEADS * HEAD_DIM, N_KV_HEADS * HEAD_DIM, D_INNER, C_XBC,
                      2 * SSD_HEADS, D_MODEL, D_MODEL])
    seg = [w[:, offs[i]:offs[i + 1]] for i in range(8)]
    w_dt = jnp.pad(seg[5], ((0, 0), (0, LANES - 2 * SSD_HEADS)))

    def head_w_spec(tm, tn):
        return [pl.BlockSpec((1, HEAD_DIM), lambda m, j: (0, 0))]

    qw = (q_norm_w.astype(F32) * (1.0 / np.sqrt(HEAD_DIM))).reshape(1, HEAD_DIM)
    kw = k_norm_w.astype(F32).reshape(1, HEAD_DIM)
    out = {}
    tm, tn = LINEAR_TILE_M, LINEAR_TILE_N
    out['q'] = _linear([h], [seg[0]], [qw], head_w_spec, _ep_head_norm, BF16, tm, tn, "proj_q")
    out['k'] = _linear([h], [seg[1]], [kw], head_w_spec, _ep_head_norm, BF16, tm, tn, "proj_k")
    out['v'] = _linear([h], [seg[2]], [], _no_aux, _ep_cast, BF16, tm, tn, "proj_v")
    out['silu_z'] = _linear([h], [seg[3]], [], _no_aux, _ep_silu, BF16, tm, tn, "proj_z")
    out['xbc'] = _linear([h], [seg[4]], [], _no_aux, _ep_cast, BF16, tm, tn, "proj_xbc")
    out['dt'] = _linear([h], [w_dt], [], _no_aux, _ep_cast, F32, tm, tn, "proj_dt")
    out['gate_attn'] = _linear([h], [seg[6]], [], _no_aux, _ep_sigmoid, BF16, tm, tn, "proj_gate_attn")
    out['gate_ssd'] = _linear([h], [seg[7]], [], _no_aux, _ep_sigmoid, BF16, tm, tn, "proj_gate_ssd")
    return out


def _pre_attention(x_prompt, x_sample, p):
    xp = x_prompt.reshape(-1, D_MODEL)
    xs = x_sample.reshape(-1, D_MODEL)
    seq_lengths = [x_prompt.shape[1]] * x_prompt.shape[0] + [x_sample.shape[1]] * x_sample.shape[0]
    h = _norm1(xp, xs, p['norm1_w'][0])
    proj = _in_projections(h, p['w_in'][0], p['q_norm_w'][0], p['k_norm_w'][0])
    tq = _pick_tile(int(np.gcd.reduce(seq_lengths)), 512)
    flags = _sequence_flags(seq_lengths, tq)
    proj['attn'] = _attention(proj['q'], proj['k'], proj['v'], p['attn_sink'][0], flags, tq)
    proj['seq_lengths'] = seq_lengths
    if 'conv_w' in p:
        proj['ssd'] = _ssd(proj, seq_lengths, p['conv_w'][0], p['conv_b'][0], p['dt_bias'][0], p['a_log'][0],
                           p['d_skip'][0], p['ssd_norm_w'][0])
    return proj


def kernel(x_prompt, x_sample, norm1_w, w_in, q_norm_w, k_norm_w, attn_sink, conv_w, conv_b, dt_bias, a_log, d_skip, ssd_norm_w, w_attn_proj, w_ssd_proj, w_out, norm2_w, router_w, router_b, w_gate_up, b_gate_up, w_down, b_down):
    assert norm1_w.shape[0] == 1, "single-layer block"
    p = dict(norm1_w=norm1_w, w_in=w_in, q_norm_w=q_norm_w, k_norm_w=k_norm_w, attn_sink=attn_sink,
             conv_w=conv_w, conv_b=conv_b, dt_bias=dt_bias, a_log=a_log, d_skip=d_skip, ssd_norm_w=ssd_norm_w)
    xp = x_prompt.reshape(-1, D_MODEL)
    xs = x_sample.reshape(-1, D_MODEL)
    tp, ts = xp.shape[0], xs.shape[0]
    t = tp + ts
    pre = _pre_attention(x_prompt, x_sample, p)

    def gate_specs(tm, tn):
        return [pl.BlockSpec((tm, tn), lambda m, j: (m, j))] * 2

    merged = _linear([pre['attn'], pre['ssd']], [w_attn_proj[0].astype(BF16), w_ssd_proj[0].astype(BF16)],
                     [pre['gate_attn'], pre['gate_ssd']], gate_specs, _ep_gated_sum, BF16, LINEAR_TILE_M,
                     LINEAR_TILE_N, "branch_merge")
    x1, h2, top_idx, top_w, counts = _out_router(merged, w_out[0], xp, xs, norm2_w[0], router_w[0], router_b[0])

    n_blocks = -(-(t * TOP_K) // MOE_ROWS) + N_EXPERTS
    tok_sorted, block_expert, n_used, dest = _route(top_idx[:, :TOP_K], top_idx[:, RANK_LANE:RANK_LANE + TOP_K],
                                                    counts[0, :N_EXPERTS].astype(jnp.int32), n_blocks)
    expert_rows = _moe(h2, tok_sorted, block_expert, n_used, w_gate_up[0], b_gate_up[0], w_down[0], b_down[0])
    yp, ys = _combine(x1, expert_rows, dest, top_w, tp, ts)
    return yp.reshape(x_prompt.shape), ys.reshape(x_sample.shape)
```

```python
import functools

import numpy as np
import jax
import jax.numpy as jnp
from jax import lax
from jax.experimental import pallas as pl
from jax.experimental.pallas import tpu as pltpu

F32 = jnp.float32
BF16 = jnp.bfloat16
HIGHEST = lax.Precision.HIGHEST

D_MODEL = 2048
N_HEADS = 16
N_KV_HEADS = 4
Q_PER_KV = N_HEADS // N_KV_HEADS
HEAD_DIM = 128
WINDOW = 128
ATTN_BLOCK = 128
D_INNER = 2048
SSD_HEAD_DIM = 64
SSD_HEADS = D_INNER // SSD_HEAD_DIM
SSD_GROUPS = 4
HEADS_PER_GROUP = SSD_HEADS // SSD_GROUPS
GROUP_WIDTH = D_INNER // SSD_GROUPS
D_STATE = 128
CONV_K = 5
SSD_CHUNK = 128
C_XBC = D_INNER + 2 * SSD_GROUPS * D_STATE
N_EXPERTS = 32
TOP_K = 4
D_FF = D_MODEL
SWIGLU_LIMIT = 7.0
SWIGLU_ALPHA = 1.702
NORM_EPS = 1e-6
MASK_VALUE = -1e30

LANES = 128
BF16_SUBLANES = 16
MIB = 1 << 20


def _cparams(semantics, vmem_mib):
    return pltpu.CompilerParams(dimension_semantics=semantics, vmem_limit_bytes=vmem_mib * MIB)


def _sigmoid(x):
    return 1.0 / (1.0 + jnp.exp(-x))


def _pick_tile(total, preferred):
    t = min(total, preferred)
    while total % t:
        t //= 2
    return t


def _norm1_kernel(xp_ref, xs_ref, w_ref, o_ref, *, n_prompt_tiles):
    m = pl.program_id(0)

    def body(x_ref):
        x = x_ref[...]
        ms = jnp.mean(x * x, axis=-1, keepdims=True)
        o_ref[...] = (x * lax.rsqrt(ms + NORM_EPS) * w_ref[...]).astype(BF16)

    @pl.when(m < n_prompt_tiles)
    def _():
        body(xp_ref)

    @pl.when(m >= n_prompt_tiles)
    def _():
        body(xs_ref)


def _two_group_specs(tile, n_prompt_tiles, n_sample_tiles, width):
    p_spec = pl.BlockSpec((tile, width), lambda m: (jnp.minimum(m, n_prompt_tiles - 1), 0))
    s_spec = pl.BlockSpec((tile, width), lambda m: (jnp.maximum(m - n_prompt_tiles, 0), 0))
    return p_spec, s_spec


def _norm1(xp, xs, w):
    tp, ts = xp.shape[0], xs.shape[0]
    tile = _pick_tile(int(np.gcd(tp, ts)), 512)
    npt, nst = tp // tile, ts // tile
    p_spec, s_spec = _two_group_specs(tile, npt, nst, D_MODEL)
    return pl.pallas_call(
        functools.partial(_norm1_kernel, n_prompt_tiles=npt),
        grid=(npt + nst,),
        in_specs=[p_spec, s_spec, pl.BlockSpec((1, D_MODEL), lambda m: (0, 0))],
        out_specs=pl.BlockSpec((tile, D_MODEL), lambda m: (m, 0)),
        out_shape=jax.ShapeDtypeStruct((tp + ts, D_MODEL), BF16),
        compiler_params=_cparams(("parallel",), 40),
        name="norm1",
    )(xp, xs, w.reshape(1, D_MODEL))


LINEAR_TILE_M = 1024
LINEAR_TILE_N = 1024


def _linear_kernel(*refs, n_lhs, n_aux, epilogue):
    lhs = refs[:n_lhs]
    rhs = refs[n_lhs:2 * n_lhs]
    aux = refs[2 * n_lhs:2 * n_lhs + n_aux]
    out = refs[2 * n_lhs + n_aux]
    accs = [jnp.dot(l[...], r[...], preferred_element_type=F32) for l, r in zip(lhs, rhs)]
    epilogue(accs, aux, out)


def _ep_cast(accs, aux, out):
    out[...] = accs[0].astype(out.dtype)


def _ep_silu(accs, aux, out):
    a = accs[0]
    out[...] = (a * _sigmoid(a)).astype(out.dtype)


def _ep_sigmoid(accs, aux, out):
    out[...] = _sigmoid(accs[0]).astype(out.dtype)


def _ep_head_norm(accs, aux, out):
    a = accs[0]
    w = aux[0][...]
    for j in range(a.shape[1] // HEAD_DIM):
        s = a[:, j * HEAD_DIM:(j + 1) * HEAD_DIM]
        ms = jnp.mean(s * s, axis=-1, keepdims=True)
        out[:, j * HEAD_DIM:(j + 1) * HEAD_DIM] = (s * lax.rsqrt(ms + NORM_EPS) * w).astype(out.dtype)


def _ep_gated_sum(accs, aux, out):
    out[...] = (aux[0][...].astype(F32) * accs[0] + aux[1][...].astype(F32) * accs[1]).astype(out.dtype)


def _linear(lhs_list, rhs_list, aux_list, aux_specs, epilogue, out_dtype, tm, tn, name):
    t, k = lhs_list[0].shape
    n = rhs_list[0].shape[1]
    tm = _pick_tile(t, tm)
    tn = _pick_tile(n, tn)
    in_specs = ([pl.BlockSpec((tm, k), lambda m, j: (m, 0)) for _ in lhs_list]
                + [pl.BlockSpec((k, tn), lambda m, j: (0, j)) for _ in rhs_list]
                + list(aux_specs(tm, tn)))
    return pl.pallas_call(
        functools.partial(_linear_kernel, n_lhs=len(lhs_list), n_aux=len(aux_list), epilogue=epilogue),
        grid=(t // tm, n // tn),
        in_specs=in_specs,
        out_specs=pl.BlockSpec((tm, tn), lambda m, j: (m, j)),
        out_shape=jax.ShapeDtypeStruct((t, n), out_dtype),
        compiler_params=_cparams(("parallel", "arbitrary"), 56),
        name=name,
    )(*lhs_list, *rhs_list, *aux_list)


def _no_aux(tm, tn):
    return []


def _attn_kernel(flags_ref, slope_ref, sink_ref, q_ref, k_ref, kp_ref, kn_ref, v_ref, vp_ref, vn_ref, o_ref,
                 *, n_sub):
    i = pl.program_id(0)
    g = pl.program_id(1)
    has_prev = flags_ref[0, i] == 0
    has_next = flags_ref[1, i] == 0

    qi = lax.broadcasted_iota(jnp.int32, (ATTN_BLOCK, 3 * ATTN_BLOCK), 0)
    kj = lax.broadcasted_iota(jnp.int32, (ATTN_BLOCK, 3 * ATTN_BLOCK), 1)
    dist = jnp.abs(ATTN_BLOCK + qi - kj)
    in_window = dist <= WINDOW
    dist_f = dist.astype(F32)
    is_prev_blk = kj < ATTN_BLOCK
    is_next_blk = kj >= 2 * ATTN_BLOCK

    for j in range(n_sub):
        rows = slice(j * ATTN_BLOCK, (j + 1) * ATTN_BLOCK)
        prev_rows = slice((j - 1) * ATTN_BLOCK, j * ATTN_BLOCK)
        next_rows = slice((j + 1) * ATTN_BLOCK, (j + 2) * ATTN_BLOCK)
        k_prev = kp_ref[...] if j == 0 else k_ref[prev_rows, :]
        v_prev = vp_ref[...] if j == 0 else v_ref[prev_rows, :]
        k_next = kn_ref[...] if j == n_sub - 1 else k_ref[next_rows, :]
        v_next = vn_ref[...] if j == n_sub - 1 else v_ref[next_rows, :]
        k_band = jnp.concatenate([k_prev, k_ref[rows, :], k_next], axis=0)
        v_band = jnp.concatenate([v_prev, v_ref[rows, :], v_next], axis=0)
        valid = in_window
        if j == 0:
            valid = valid & (has_prev | jnp.logical_not(is_prev_blk))
        if j == n_sub - 1:
            valid = valid & (has_next | jnp.logical_not(is_next_blk))
        q_stack = jnp.concatenate(
            [q_ref[rows, r * HEAD_DIM:(r + 1) * HEAD_DIM] for r in range(Q_PER_KV)], axis=0)
        s_all = lax.dot_general(q_stack, k_band, (((1,), (1,)), ((), ())), preferred_element_type=F32)
        for r in range(Q_PER_KV):
            head = g * Q_PER_KV + r
            s = s_all[r * ATTN_BLOCK:(r + 1) * ATTN_BLOCK, :] - slope_ref[head] * dist_f
            s = jnp.where(valid, s, MASK_VALUE)
            sink = sink_ref[head]
            m = jnp.maximum(jnp.max(s, axis=-1, keepdims=True), sink)
            p = jnp.exp(s - m)
            denom = jnp.sum(p, axis=-1, keepdims=True) + jnp.exp(sink - m)
            o = jnp.dot(p.astype(BF16), v_band, preferred_element_type=F32)
            o_ref[rows, r * HEAD_DIM:(r + 1) * HEAD_DIM] = (o / denom).astype(o_ref.dtype)


def _attention(q, k, v, sink, chunk_flags, tq):
    t = q.shape[0]
    n_chunks = t // tq
    n_blocks = t // ATTN_BLOCK
    sub = tq // ATTN_BLOCK
    slopes = jnp.asarray(2.0 ** (-8.0 * (np.arange(N_HEADS, dtype=np.float32) + 1.0) / N_HEADS), F32)
    gw = Q_PER_KV * HEAD_DIM

    def own(width):
        return pl.BlockSpec((tq, width), lambda i, g, *_: (i, g))

    prev = pl.BlockSpec((ATTN_BLOCK, HEAD_DIM), lambda i, g, *_: (jnp.maximum(i * sub - 1, 0), g))
    nxt = pl.BlockSpec((ATTN_BLOCK, HEAD_DIM), lambda i, g, *_: (jnp.minimum((i + 1) * sub, n_blocks - 1), g))
    grid_spec = pltpu.PrefetchScalarGridSpec(
        num_scalar_prefetch=3,
        grid=(n_chunks, N_KV_HEADS),
        in_specs=[own(gw), own(HEAD_DIM), prev, nxt, own(HEAD_DIM), prev, nxt],
        out_specs=own(gw),
    )
    return pl.pallas_call(
        functools.partial(_attn_kernel, n_sub=sub),
        grid_spec=grid_spec,
        out_shape=jax.ShapeDtypeStruct((t, N_HEADS * HEAD_DIM), BF16),
        compiler_params=_cparams(("parallel", "arbitrary"), 32),
        name="banded_attention",
    )(chunk_flags, slopes, sink.astype(F32), q, k, k, k, v, v, v)


CONV_HALO = BF16_SUBLANES
CONV_PAD = CONV_K // 2
CONV_TAP_ROWS = 8
CONV_COL_TILE = 512
HEAD_PAIR_WIDTH = 2 * SSD_HEAD_DIM


def _ssd_kernel(flags_ref, *rest, reverse):
    i = pl.program_id(0)
    c = pl.num_programs(0) - 1 - i if reverse else i
    seq_first = flags_ref[0, c] == 1
    seq_last = flags_ref[1, c] == 1
    L = SSD_CHUNK

    if reverse:
        xc_ref, dt_ref, dtb_ref, alog_ref, yf_ref, sz_ref, dskip_ref, normw_ref, out_ref, y_scr, state_scr = rest

        def xc(cols):
            return xc_ref[:, cols].astype(F32)
    else:
        (xbc_ref, xprev_ref, xnext_ref, dt_ref, convw_ref, convb_ref, dtb_ref, alog_ref, out_ref, xc_out_ref,
         ext_scr, xc_scr, y_scr, state_scr) = rest

        def xc(cols):
            return xc_scr[:, cols]

        zero_halo = jnp.zeros((CONV_HALO, C_XBC), BF16)
        ext_scr[0:CONV_HALO, :] = jnp.where(seq_first, zero_halo, xprev_ref[...])
        ext_scr[CONV_HALO:CONV_HALO + L, :] = xbc_ref[...]
        ext_scr[CONV_HALO + L:, :] = jnp.where(seq_last, zero_halo, xnext_ref[...])
        out_row = lax.broadcasted_iota(jnp.int32, (L, L + 2 * CONV_HALO), 0)
        src_row = lax.broadcasted_iota(jnp.int32, (L, L + 2 * CONV_HALO), 1)
        for ct in range(C_XBC // CONV_COL_TILE):
            cols = slice(ct * CONV_COL_TILE, (ct + 1) * CONV_COL_TILE)
            ext = ext_scr[:, cols]
            acc = convb_ref[:, cols] + xbc_ref[:, cols].astype(F32) * convw_ref[CONV_PAD:CONV_PAD + 1, cols]
            for j in range(CONV_K):
                if j == CONV_PAD:
                    continue
                shift = (src_row == out_row + (CONV_HALO - CONV_PAD + j)).astype(BF16)
                acc = acc + jnp.dot(shift, ext, preferred_element_type=F32) * convw_ref[j:j + 1, cols]
            conv = acc * _sigmoid(acc)
            xc_scr[:, cols] = conv
            xc_out_ref[:, cols] = conv.astype(xc_out_ref.dtype)

    col = lax.broadcasted_iota(jnp.int32, (1, LANES), 1)
    a_neg = jnp.where(col < 2 * SSD_HEADS, -jnp.exp(alog_ref[...]), 0.0)
    xdt = dt_ref[...] + dtb_ref[...]
    dt = jnp.maximum(xdt, 0.0) + jnp.log1p(jnp.exp(-jnp.abs(xdt)))
    a = dt * a_neg
    ri = lax.broadcasted_iota(jnp.int32, (L, L), 0)
    ci = lax.broadcasted_iota(jnp.int32, (L, L), 1)
    causal = (ri <= ci) if reverse else (ri >= ci)
    a_cum = jnp.dot(causal.astype(F32), a, precision=HIGHEST, preferred_element_type=F32)
    a_cum_t = a_cum.T
    dt_t = dt.T
    edge = 0 if reverse else L - 1
    a_total = a_cum[edge:edge + 1, :]
    w_state = dt * jnp.exp(a_total - a_cum)
    dir_off = SSD_HEADS if reverse else 0
    hr = lax.broadcasted_iota(jnp.int32, (LANES, D_INNER), 0)
    hc = lax.broadcasted_iota(jnp.int32, (LANES, D_INNER), 1)
    expand = (hr == dir_off + lax.shift_right_logical(hc, int(np.log2(SSD_HEAD_DIM)))).astype(F32)
    chunk_decay = jnp.dot(jnp.broadcast_to(jnp.exp(a_total), (8, LANES)), expand, precision=HIGHEST,
                          preferred_element_type=F32)[0:1, :]

    @pl.when(seq_last if reverse else seq_first)
    def _():
        state_scr[...] = jnp.zeros_like(state_scr)

    lo = lax.broadcasted_iota(jnp.int32, (L, LANES), 1) < SSD_HEAD_DIM
    for g in range(SSD_GROUPS):
        b_g = xc(slice(D_INNER + g * D_STATE, D_INNER + (g + 1) * D_STATE))
        c_g = xc(slice(D_INNER + (SSD_GROUPS + g) * D_STATE, D_INNER + (SSD_GROUPS + g + 1) * D_STATE))
        c_bf = c_g.astype(BF16)
        cb = lax.dot_general(c_bf, b_g.astype(BF16), (((1,), (1,)), ((), ())), preferred_element_type=F32)
        b_t = b_g.T.astype(BF16)
        state = state_scr[g]
        y_off = jnp.dot(c_bf, state.astype(BF16), preferred_element_type=F32)
        xw_parts = []
        for p in range(HEADS_PER_GROUP // 2):
            gcols = slice(g * GROUP_WIDTH + p * HEAD_PAIR_WIDTH, g * GROUP_WIDTH + (p + 1) * HEAD_PAIR_WIDTH)
            m_parts, e_parts, w_parts = [], [], []
            for hh in range(2):
                k = dir_off + g * HEADS_PER_GROUP + 2 * p + hh
                colb = jnp.broadcast_to(a_cum[:, k:k + 1], (L, L))
                rowb = jnp.broadcast_to(a_cum_t[k:k + 1, :], (L, L))
                dtrow = jnp.broadcast_to(dt_t[k:k + 1, :], (L, L))
                decay = jnp.exp(jnp.where(causal, colb - rowb, -jnp.inf))
                m_parts.append((cb * decay * dtrow).astype(BF16))
                e_parts.append(jnp.exp(colb))
                w_parts.append(jnp.broadcast_to(w_state[:, k:k + 1], (L, LANES)))
            x_pair = xc(gcols)
            rhs = jnp.concatenate([jnp.where(lo, x_pair, 0.0), jnp.where(lo, 0.0, x_pair)], axis=0).astype(BF16)
            y = jnp.dot(jnp.concatenate(m_parts, axis=1), rhs, preferred_element_type=F32)
            y = y + y_off[:, p * HEAD_PAIR_WIDTH:(p + 1) * HEAD_PAIR_WIDTH] * jnp.where(lo, e_parts[0], e_parts[1])
            y_scr[:, gcols] = y
            xw_parts.append((x_pair * jnp.where(lo, w_parts[0], w_parts[1])).astype(BF16))
        xw = jnp.concatenate(xw_parts, axis=1)
        state_scr[g] = (state * chunk_decay[:, g * GROUP_WIDTH:(g + 1) * GROUP_WIDTH]
                        + jnp.dot(b_t, xw, preferred_element_type=F32))

    if not reverse:
        out_ref[...] = y_scr[...].astype(out_ref.dtype)
    else:
        for g in range(SSD_GROUPS):
            cols = slice(g * GROUP_WIDTH, (g + 1) * GROUP_WIDTH)
            y = y_scr[:, cols] + yf_ref[:, cols].astype(F32) + xc(cols) * dskip_ref[:, cols]
            yg = y * sz_ref[:, cols].astype(F32)
            ms = jnp.mean(yg * yg, axis=-1, keepdims=True)
            out_ref[:, cols] = (yg * lax.rsqrt(ms + NORM_EPS) * normw_ref[:, cols]).astype(out_ref.dtype)


def _ssd(proj, seq_lengths, conv_w, conv_b, dt_bias, a_log, d_skip, ssd_norm_w):
    xbc, dt = proj['xbc'], proj['dt']
    t = xbc.shape[0]
    n_chunks = t // SSD_CHUNK
    halo_per_chunk = SSD_CHUNK // CONV_HALO
    n_halo_blocks = t // CONV_HALO
    flags = _sequence_flags(seq_lengths, SSD_CHUNK)
    pad = LANES - 2 * SSD_HEADS
    dtb = jnp.pad(dt_bias.astype(F32).reshape(1, -1), ((0, 0), (0, pad)))
    alog = jnp.pad(a_log.astype(F32).reshape(1, -1), ((0, 0), (0, pad)))
    scan_scratch = [pltpu.VMEM((SSD_CHUNK, D_INNER), F32), pltpu.VMEM((SSD_GROUPS, D_STATE, GROUP_WIDTH), F32)]

    def const(shape):
        return pl.BlockSpec(shape, lambda i, *_: (0, 0))

    def row(width):
        return pl.BlockSpec((SSD_CHUNK, width), lambda i, *_: (i, 0))

    def rev_row(width):
        return pl.BlockSpec((SSD_CHUNK, width), lambda i, *_: (n_chunks - 1 - i, 0))

    prev = pl.BlockSpec((CONV_HALO, C_XBC), lambda i, *_: (jnp.maximum(i * halo_per_chunk - 1, 0), 0))
    nxt = pl.BlockSpec((CONV_HALO, C_XBC),
                       lambda i, *_: (jnp.minimum((i + 1) * halo_per_chunk, n_halo_blocks - 1), 0))
    y_fwd, xc = pl.pallas_call(
        functools.partial(_ssd_kernel, reverse=False),
        grid_spec=pltpu.PrefetchScalarGridSpec(
            num_scalar_prefetch=1,
            grid=(n_chunks,),
            in_specs=[row(C_XBC), prev, nxt, row(LANES), const((CONV_TAP_ROWS, C_XBC)), const((1, C_XBC)),
                      const((1, LANES)), const((1, LANES))],
            out_specs=[row(D_INNER), row(C_XBC)],
            scratch_shapes=[pltpu.VMEM((SSD_CHUNK + 2 * CONV_HALO, C_XBC), BF16),
                            pltpu.VMEM((SSD_CHUNK, C_XBC), F32)] + scan_scratch,
        ),
        out_shape=[jax.ShapeDtypeStruct((t, D_INNER), BF16), jax.ShapeDtypeStruct((t, C_XBC), BF16)],
        compiler_params=_cparams(("arbitrary",), 40),
        name="ssd_fwd",
    )(flags, xbc, xbc, xbc, dt, jnp.pad(conv_w.astype(F32), ((0, CONV_TAP_ROWS - CONV_K), (0, 0))),
      conv_b.astype(F32).reshape(1, C_XBC), dtb, alog)

    d_lanes = jnp.repeat(d_skip.astype(F32), SSD_HEAD_DIM).reshape(1, D_INNER)
    return pl.pallas_call(
        functools.partial(_ssd_kernel, reverse=True),
        grid_spec=pltpu.PrefetchScalarGridSpec(
            num_scalar_prefetch=1,
            grid=(n_chunks,),
            in_specs=[rev_row(C_XBC), rev_row(LANES), const((1, LANES)), const((1, LANES)), rev_row(D_INNER),
                      rev_row(D_INNER), const((1, D_INNER)), const((1, D_INNER))],
            out_specs=rev_row(D_INNER),
            scratch_shapes=scan_scratch,
        ),
        out_shape=jax.ShapeDtypeStruct((t, D_INNER), BF16),
        compiler_params=_cparams(("arbitrary",), 40),
        name="ssd_bwd",
    )(flags, xc, dt, dtb, alog, y_fwd, proj['silu_z'], d_lanes, ssd_norm_w.astype(F32).reshape(1, D_INNER))


RANK_LANE = TOP_K


def _out_router_kernel(m_ref, w_ref, xp_ref, xs_ref, nw_ref, rw_ref, rb_ref, x1_ref, h2_ref, ti_ref, tw_ref,
                       cnt_ref, *, n_prompt_tiles):
    i = pl.program_id(0)
    acc = jnp.dot(m_ref[...], w_ref[...], preferred_element_type=F32)

    @pl.when(i == 0)
    def _():
        cnt_ref[...] = jnp.zeros_like(cnt_ref)

    def finish(x):
        x1 = x + acc
        x1_ref[...] = x1
        ms = jnp.mean(x1 * x1, axis=-1, keepdims=True)
        h2 = x1 * lax.rsqrt(ms + NORM_EPS) * nw_ref[...]
        h2_ref[...] = h2
        v = jnp.dot(h2.astype(BF16), rw_ref[...], preferred_element_type=F32) + rb_ref[...]
        lane = lax.broadcasted_iota(jnp.int32, v.shape, 1)
        vals, idxs = [], []
        for _ in range(TOP_K):
            top = jnp.max(v, axis=-1, keepdims=True)
            idx = jnp.min(jnp.where(v == top, lane, LANES), axis=-1, keepdims=True)
            vals.append(top)
            idxs.append(idx)
            v = jnp.where(lane == idx, -jnp.inf, v)
        es = [jnp.exp(val - vals[0]) for val in vals]
        total = es[0]
        for e in es[1:]:
            total = total + e
        n_rows = v.shape[0]
        chosen = jnp.zeros(v.shape, F32)
        for kk in range(TOP_K):
            chosen = chosen + (lane == idxs[kk]).astype(F32)
        ri = lax.broadcasted_iota(jnp.int32, (n_rows, n_rows), 0)
        ci = lax.broadcasted_iota(jnp.int32, (n_rows, n_rows), 1)
        before = jnp.dot((ci < ri).astype(BF16), chosen.astype(BF16), preferred_element_type=F32)
        before = before + cnt_ref[0:1, :]
        cnt_ref[...] = cnt_ref[...] + jnp.sum(chosen, axis=0, keepdims=True)
        ti = jnp.zeros(v.shape, jnp.int32)
        tw = jnp.zeros(v.shape, F32)
        for kk in range(TOP_K):
            rank = jnp.sum(jnp.where(lane == idxs[kk], before, 0.0), axis=-1, keepdims=True)
            ti = jnp.where(lane == kk, idxs[kk], ti)
            ti = jnp.where(lane == RANK_LANE + kk, rank.astype(jnp.int32), ti)
            tw = jnp.where(lane == kk, es[kk] / total, tw)
        ti_ref[...] = ti
        tw_ref[...] = tw

    finish(jnp.where(i < n_prompt_tiles, xp_ref[...], xs_ref[...]))


def _out_router(merged, w_out, xp, xs, norm2_w, router_w, router_b):
    tp, ts = xp.shape[0], xs.shape[0]
    t = tp + ts
    tile = _pick_tile(int(np.gcd(tp, ts)), 256)
    npt, nst = tp // tile, ts // tile
    p_spec, s_spec = _two_group_specs(tile, npt, nst, D_MODEL)
    rw = jnp.pad(router_w.astype(BF16), ((0, 0), (0, LANES - N_EXPERTS)))
    rb = jnp.pad(router_b.astype(F32).reshape(1, N_EXPERTS), ((0, 0), (0, LANES - N_EXPERTS)),
                 constant_values=-jnp.inf)

    def const(shape):
        return pl.BlockSpec(shape, lambda m: (0, 0))

    def row(width):
        return pl.BlockSpec((tile, width), lambda m: (m, 0))

    return pl.pallas_call(
        functools.partial(_out_router_kernel, n_prompt_tiles=npt),
        grid=(npt + nst,),
        in_specs=[row(D_MODEL), const((D_MODEL, D_MODEL)), p_spec, s_spec, const((1, D_MODEL)),
                  const((D_MODEL, LANES)), const((1, LANES))],
        out_specs=[row(D_MODEL), row(D_MODEL), row(LANES), row(LANES), const((8, LANES))],
        out_shape=[jax.ShapeDtypeStruct((t, D_MODEL), F32), jax.ShapeDtypeStruct((t, D_MODEL), F32),
                   jax.ShapeDtypeStruct((t, LANES), jnp.int32), jax.ShapeDtypeStruct((t, LANES), F32),
                   jax.ShapeDtypeStruct((8, LANES), F32)],
        compiler_params=_cparams(("arbitrary",), 48),
        name="out_proj_router",
    )(merged, w_out.astype(BF16), xp, xs, norm2_w.astype(F32).reshape(1, D_MODEL), rw, rb)


MOE_ROWS = 512
MOE_FF_TILE = 1024


def _moe_kernel(be_ref, nb_ref, tok_hbm, h2_hbm, wg_ref, wu_ref, bg_ref, bu_ref, wd_ref, bd_ref, out_ref,
                idx_smem, xbuf, xb_scr, idx_sem, row_sem):
    b = pl.program_id(0)
    f = pl.program_id(1)
    n_f = pl.num_programs(1)
    n_used = nb_ref[0]

    def idx_copy(block, slot):
        return pltpu.make_async_copy(tok_hbm.at[block], idx_smem.at[slot], idx_sem.at[slot])

    def issue_rows(slot):
        def body(r, carry):
            tok = idx_smem[slot, r]
            pltpu.make_async_copy(h2_hbm.at[pl.ds(tok, 1), :], xbuf.at[slot, pl.ds(r, 1), :],
                                  row_sem.at[slot]).start()
            return carry
        lax.fori_loop(0, MOE_ROWS, body, 0, unroll=True)

    def wait_rows(slot):
        pltpu.make_async_copy(h2_hbm.at[pl.ds(0, MOE_ROWS), :], xbuf.at[slot], row_sem.at[slot]).wait()

    @pl.when(b < n_used)
    def _():
        slot = lax.rem(b, 2)
        nslot = 1 - slot

        @pl.when(f == 0)
        def _():
            @pl.when(b == 0)
            def _():
                idx_copy(0, 0).start()
                idx_copy(0, 0).wait()
                issue_rows(0)

                @pl.when(n_used > 1)
                def _():
                    idx_copy(1, 1).start()

            @pl.when(b + 1 < n_used)
            def _():
                idx_copy(b + 1, nslot).wait()
                for s in range(2):
                    @pl.when(nslot == s)
                    def _():
                        issue_rows(s)

                @pl.when(b + 2 < n_used)
                def _():
                    idx_copy(b + 2, slot).start()

            wait_rows(slot)
            xb_scr[...] = xbuf[slot].astype(BF16)

        xb = xb_scr[...]
        gate = jnp.dot(xb, wg_ref[0], preferred_element_type=F32) + bg_ref[0]
        up = jnp.dot(xb, wu_ref[0], preferred_element_type=F32) + bu_ref[0]
        gate = jnp.minimum(gate, SWIGLU_LIMIT)
        up = jnp.clip(up, -SWIGLU_LIMIT, SWIGLU_LIMIT)
        act = (gate * _sigmoid(SWIGLU_ALPHA * gate) * (up + 1.0)).astype(BF16)
        base = jnp.where(f == 0, jnp.broadcast_to(bd_ref[0], out_ref.shape), out_ref[...])
        out_ref[...] = base + jnp.dot(act, wd_ref[0], preferred_element_type=F32)

    @pl.when(jnp.logical_and(b >= n_used, f == n_f - 1))
    def _():
        out_ref[...] = jnp.zeros_like(out_ref)


def _moe(h2, tok_sorted, block_expert, n_used, w_gate_up, b_gate_up, w_down, b_down):
    n_blocks = tok_sorted.shape[0]
    n_f = D_FF // MOE_FF_TILE
    wgu = w_gate_up.astype(BF16)
    wd = w_down.astype(BF16)
    bgu = b_gate_up.astype(F32).reshape(N_EXPERTS, 1, 2 * D_FF)
    bd = b_down.astype(F32).reshape(N_EXPERTS, 1, D_MODEL)

    def live(b, nb):
        return b < nb[0]

    def ff(b, f, nb):
        return jnp.where(live(b, nb), f, n_f - 1)

    grid_spec = pltpu.PrefetchScalarGridSpec(
        num_scalar_prefetch=2,
        grid=(n_blocks, n_f),
        in_specs=[
            pl.BlockSpec(memory_space=pl.ANY),
            pl.BlockSpec(memory_space=pl.ANY),
            pl.BlockSpec((1, D_MODEL, MOE_FF_TILE), lambda b, f, be, nb: (be[b], 0, ff(b, f, nb))),
            pl.BlockSpec((1, D_MODEL, MOE_FF_TILE), lambda b, f, be, nb: (be[b], 0, n_f + ff(b, f, nb))),
            pl.BlockSpec((1, 1, MOE_FF_TILE), lambda b, f, be, nb: (be[b], 0, ff(b, f, nb))),
            pl.BlockSpec((1, 1, MOE_FF_TILE), lambda b, f, be, nb: (be[b], 0, n_f + ff(b, f, nb))),
            pl.BlockSpec((1, MOE_FF_TILE, D_MODEL), lambda b, f, be, nb: (be[b], ff(b, f, nb), 0)),
            pl.BlockSpec((1, 1, D_MODEL), lambda b, f, be, nb: (be[b], 0, 0)),
        ],
        out_specs=pl.BlockSpec((MOE_ROWS, D_MODEL), lambda b, f, be, nb: (b, 0)),
        scratch_shapes=[
            pltpu.SMEM((2, MOE_ROWS), jnp.int32),
            pltpu.VMEM((2, MOE_ROWS, D_MODEL), F32),
            pltpu.VMEM((MOE_ROWS, D_MODEL), BF16),
            pltpu.SemaphoreType.DMA((2,)),
            pltpu.SemaphoreType.DMA((2,)),
        ],
    )
    return pl.pallas_call(
        _moe_kernel,
        grid_spec=grid_spec,
        out_shape=jax.ShapeDtypeStruct((n_blocks * MOE_ROWS, D_MODEL), F32),
        compiler_params=_cparams(("arbitrary", "arbitrary"), 56),
        name="moe_experts",
    )(block_expert, n_used, tok_sorted, h2, wgu, wgu, bgu, bgu, wd, bd)


def _route(top_idx, rank, counts, n_blocks):
    t = top_idx.shape[0]
    flat_e = top_idx.reshape(-1)
    rank = rank.reshape(-1)
    blocks_e = (counts + MOE_ROWS - 1) // MOE_ROWS
    blocks_end = jnp.cumsum(blocks_e)
    blocks_start = blocks_end - blocks_e
    dest = (blocks_start[flat_e] * MOE_ROWS + rank).astype(jnp.int32)
    n_used = blocks_end[-1].astype(jnp.int32)
    flat_tok = jnp.arange(t * TOP_K, dtype=jnp.int32) // TOP_K
    tok_sorted = jnp.zeros((n_blocks * MOE_ROWS,), jnp.int32).at[dest].set(flat_tok, unique_indices=True)
    bidx = jnp.arange(n_blocks, dtype=jnp.int32)
    be = jnp.minimum(jnp.sum((blocks_end[None, :] <= bidx[:, None]).astype(jnp.int32), axis=1), N_EXPERTS - 1)
    be = jnp.where(bidx < n_used, be, be[jnp.maximum(n_used - 1, 0)])
    return tok_sorted.reshape(n_blocks, MOE_ROWS), be, n_used.reshape(1), dest


COMBINE_ROWS = 128


def _combine_kernel(pos_hbm, rows_hbm, x1_ref, tw_ref, yp_ref, ys_ref, pos_smem, gbuf, pos_sem, row_sem,
                    *, n_prompt_tiles):
    i = pl.program_id(0)
    n = pl.num_programs(0)
    slot = lax.rem(i, 2)
    nslot = 1 - slot

    def pos_copy(tile, s):
        return pltpu.make_async_copy(pos_hbm.at[tile], pos_smem.at[s], pos_sem.at[s])

    def issue_rows(s):
        def body(r, carry):
            for kk in range(TOP_K):
                src = pos_smem[s, r * TOP_K + kk]
                pltpu.make_async_copy(rows_hbm.at[pl.ds(src, 1), :], gbuf.at[s, kk, pl.ds(r, 1), :],
                                      row_sem.at[s]).start()
            return carry
        lax.fori_loop(0, COMBINE_ROWS, body, 0, unroll=True)

    def wait_rows(s):
        for kk in range(TOP_K):
            pltpu.make_async_copy(rows_hbm.at[pl.ds(0, COMBINE_ROWS), :], gbuf.at[s, kk], row_sem.at[s]).wait()

    @pl.when(i == 0)
    def _():
        pos_copy(0, 0).start()
        pos_copy(0, 0).wait()
        issue_rows(0)

        @pl.when(n > 1)
        def _():
            pos_copy(1, 1).start()

    @pl.when(i + 1 < n)
    def _():
        pos_copy(i + 1, nslot).wait()
        for s in range(2):
            @pl.when(nslot == s)
            def _():
                issue_rows(s)

        @pl.when(i + 2 < n)
        def _():
            pos_copy(i + 2, slot).start()

    wait_rows(slot)
    tw = tw_ref[...]
    y = x1_ref[...]
    for kk in range(TOP_K):
        y = y + tw[:, kk:kk + 1] * gbuf[slot, kk]

    @pl.when(i < n_prompt_tiles)
    def _():
        yp_ref[...] = y

    @pl.when(i >= n_prompt_tiles)
    def _():
        ys_ref[...] = y


def _combine(x1, expert_rows, dest, top_w, tp, ts):
    t = tp + ts
    tile = _pick_tile(int(np.gcd(tp, ts)), COMBINE_ROWS)
    assert tile == COMBINE_ROWS
    npt, nst = tp // tile, ts // tile
    p_spec, s_spec = _two_group_specs(tile, npt, nst, D_MODEL)
    return pl.pallas_call(
        functools.partial(_combine_kernel, n_prompt_tiles=npt),
        grid=(npt + nst,),
        in_specs=[pl.BlockSpec(memory_space=pl.ANY), pl.BlockSpec(memory_space=pl.ANY),
                  pl.BlockSpec((tile, D_MODEL), lambda m: (m, 0)), pl.BlockSpec((tile, LANES), lambda m: (m, 0))],
        out_specs=[p_spec, s_spec],
        out_shape=[jax.ShapeDtypeStruct((tp, D_MODEL), F32), jax.ShapeDtypeStruct((ts, D_MODEL), F32)],
        scratch_shapes=[
            pltpu.SMEM((2, COMBINE_ROWS * TOP_K), jnp.int32),
            pltpu.VMEM((2, TOP_K, COMBINE_ROWS, D_MODEL), F32),
            pltpu.SemaphoreType.DMA((2,)),
            pltpu.SemaphoreType.DMA((2,)),
        ],
        compiler_params=_cparams(("arbitrary",), 32),
        name="moe_combine",
    )(dest.reshape(t // tile, tile * TOP_K), expert_rows, x1, top_w)


def _sequence_flags(seq_lengths, unit):
    first, last = [], []
    for length in seq_lengths:
        n = length // unit
        first += [1] + [0] * (n - 1)
        last += [0] * (n - 1) + [1]
    return jnp.asarray(np.array([first, last], np.int32))


def _in_projections(h, w_in, q_norm_w, k_norm_w):
    w = w_in.astype(BF16)
    offs = np.cumsum([0, N_HEADS * HEAD_DIM, N_KV_HEADS * HEAD_DIM, N_KV_HEADS * HEAD_DIM, D_INNER, C_XBC,
                      2 * SSD_HEADS, D_MODEL, D_MODEL])
    seg = [w[:, offs[i]:offs[i + 1]] for i in range(8)]
    w_dt = jnp.pad(seg[5], ((0, 0), (0, LANES - 2 * SSD_HEADS)))

    def head_w_spec(tm, tn):
        return [pl.BlockSpec((1, HEAD_DIM), lambda m, j: (0, 0))]

    qw = (q_norm_w.astype(F32) * (1.0 / np.sqrt(HEAD_DIM))).reshape(1, HEAD_DIM)
    kw = k_norm_w.astype(F32).reshape(1, HEAD_DIM)
    out = {}
    tm, tn = LINEAR_TILE_M, LINEAR_TILE_N
    out['q'] = _linear([h], [seg[0]], [qw], head_w_spec, _ep_head_norm, BF16, tm, tn, "proj_q")
    out['k'] = _linear([h], [seg[1]], [kw], head_w_spec, _ep_head_norm, BF16, tm, tn, "proj_k")
    out['v'] = _linear([h], [seg[2]], [], _no_aux, _ep_cast, BF16, tm, tn, "proj_v")
    out['silu_z'] = _linear([h], [seg[3]], [], _no_aux, _ep_silu, BF16, tm, tn, "proj_z")
    out['xbc'] = _linear([h], [seg[4]], [], _no_aux, _ep_cast, BF16, tm, tn, "proj_xbc")
    out['dt'] = _linear([h], [w_dt], [], _no_aux, _ep_cast, F32, tm, tn, "proj_dt")
    out['gate_attn'] = _linear([h], [seg[6]], [], _no_aux, _ep_sigmoid, BF16, tm, tn, "proj_gate_attn")
    out['gate_ssd'] = _linear([h], [seg[7]], [], _no_aux, _ep_sigmoid, BF16, tm, tn, "proj_gate_ssd")
    return out


def _pre_attention(x_prompt, x_sample, p):
    xp = x_prompt.reshape(-1, D_MODEL)
    xs = x_sample.reshape(-1, D_MODEL)
    seq_lengths = [x_prompt.shape[1]] * x_prompt.shape[0] + [x_sample.shape[1]] * x_sample.shape[0]
    h = _norm1(xp, xs, p['norm1_w'][0])
    proj = _in_projections(h, p['w_in'][0], p['q_norm_w'][0], p['k_norm_w'][0])
    tq = _pick_tile(int(np.gcd.reduce(seq_lengths)), 512)
    flags = _sequence_flags(seq_lengths, tq)
    proj['attn'] = _attention(proj['q'], proj['k'], proj['v'], p['attn_sink'][0], flags, tq)
    proj['seq_lengths'] = seq_lengths
    if 'conv_w' in p:
        proj['ssd'] = _ssd(proj, seq_lengths, p['conv_w'][0], p['conv_b'][0], p['dt_bias'][0], p['a_log'][0],
                           p['d_skip'][0], p['ssd_norm_w'][0])
    return proj


def kernel(x_prompt, x_sample, norm1_w, w_in, q_norm_w, k_norm_w, attn_sink, conv_w, conv_b, dt_bias, a_log, d_skip, ssd_norm_w, w_attn_proj, w_ssd_proj, w_out, norm2_w, router_w, router_b, w_gate_up, b_gate_up, w_down, b_down):
    assert norm1_w.shape[0] == 1, "single-layer block"
    p = dict(norm1_w=norm1_w, w_in=w_in, q_norm_w=q_norm_w, k_norm_w=k_norm_w, attn_sink=attn_sink,
             conv_w=conv_w, conv_b=conv_b, dt_bias=dt_bias, a_log=a_log, d_skip=d_skip, ssd_norm_w=ssd_norm_w)
    xp = x_prompt.reshape(-1, D_MODEL)
    xs = x_sample.reshape(-1, D_MODEL)
    tp, ts = xp.shape[0], xs.shape[0]
    t = tp + ts
    pre = _pre_attention(x_prompt, x_sample, p)

    def gate_specs(tm, tn):
        return [pl.BlockSpec((tm, tn), lambda m, j: (m, j))] * 2

    merged = _linear([pre['attn'], pre['ssd']], [w_attn_proj[0].astype(BF16), w_ssd_proj[0].astype(BF16)],
                     [pre['gate_attn'], pre['gate_ssd']], gate_specs, _ep_gated_sum, BF16, LINEAR_TILE_M,
                     LINEAR_TILE_N, "branch_merge")
    x1, h2, top_idx, top_w, counts = _out_router(merged, w_out[0], xp, xs, norm2_w[0], router_w[0], router_b[0])

    n_blocks = -(-(t * TOP_K) // MOE_ROWS) + N_EXPERTS
    tok_sorted, block_expert, n_used, dest = _route(top_idx[:, :TOP_K], top_idx[:, RANK_LANE:RANK_LANE + TOP_K],
                                                    counts[0, :N_EXPERTS].astype(jnp.int32), n_blocks)
    expert_rows = _moe(h2, tok_sorted, block_expert, n_used, w_gate_up[0], b_gate_up[0], w_down[0], b_down[0])
    yp, ys = _combine(x1, expert_rows, dest, top_w, tp, ts)
    return yp.reshape(x_prompt.shape), ys.reshape(x_sample.shape)
```

```python
import functools

import numpy as np
import jax
import jax.numpy as jnp
from jax import lax
from jax.experimental import pallas as pl
from jax.experimental.pallas import tpu as pltpu

F32 = jnp.float32
BF16 = jnp.bfloat16
HIGHEST = lax.Precision.HIGHEST

D_MODEL = 2048
N_HEADS = 16
N_KV_HEADS = 4
Q_PER_KV = N_HEADS // N_KV_HEADS
HEAD_DIM = 128
WINDOW = 128
ATTN_BLOCK = 128
D_INNER = 2048
SSD_HEAD_DIM = 64
SSD_HEADS = D_INNER // SSD_HEAD_DIM
SSD_GROUPS = 4
HEADS_PER_GROUP = SSD_HEADS // SSD_GROUPS
GROUP_WIDTH = D_INNER // SSD_GROUPS
D_STATE = 128
CONV_K = 5
SSD_CHUNK = 128
C_XBC = D_INNER + 2 * SSD_GROUPS * D_STATE
N_EXPERTS = 32
TOP_K = 4
D_FF = D_MODEL
SWIGLU_LIMIT = 7.0
SWIGLU_ALPHA = 1.702
NORM_EPS = 1e-6
MASK_VALUE = -1e30

LANES = 128
BF16_SUBLANES = 16
MIB = 1 << 20


def _cparams(semantics, vmem_mib):
    return pltpu.CompilerParams(dimension_semantics=semantics, vmem_limit_bytes=vmem_mib * MIB)


def _sigmoid(x):
    return 1.0 / (1.0 + jnp.exp(-x))


def _pick_tile(total, preferred):
    t = min(total, preferred)
    while total % t:
        t //= 2
    return t


def _norm1_kernel(xp_ref, xs_ref, w_ref, o_ref, *, n_prompt_tiles):
    m = pl.program_id(0)

    def body(x_ref):
        x = x_ref[...]
        ms = jnp.mean(x * x, axis=-1, keepdims=True)
        o_ref[...] = (x * lax.rsqrt(ms + NORM_EPS) * w_ref[...]).astype(BF16)

    @pl.when(m < n_prompt_tiles)
    def _():
        body(xp_ref)

    @pl.when(m >= n_prompt_tiles)
    def _():
        body(xs_ref)


def _two_group_specs(tile, n_prompt_tiles, n_sample_tiles, width):
    p_spec = pl.BlockSpec((tile, width), lambda m: (jnp.minimum(m, n_prompt_tiles - 1), 0))
    s_spec = pl.BlockSpec((tile, width), lambda m: (jnp.maximum(m - n_prompt_tiles, 0), 0))
    return p_spec, s_spec


def _norm1(xp, xs, w):
    tp, ts = xp.shape[0], xs.shape[0]
    tile = _pick_tile(int(np.gcd(tp, ts)), 512)
    npt, nst = tp // tile, ts // tile
    p_spec, s_spec = _two_group_specs(tile, npt, nst, D_MODEL)
    return pl.pallas_call(
        functools.partial(_norm1_kernel, n_prompt_tiles=npt),
        grid=(npt + nst,),
        in_specs=[p_spec, s_spec, pl.BlockSpec((1, D_MODEL), lambda m: (0, 0))],
        out_specs=pl.BlockSpec((tile, D_MODEL), lambda m: (m, 0)),
        out_shape=jax.ShapeDtypeStruct((tp + ts, D_MODEL), BF16),
        compiler_params=_cparams(("parallel",), 40),
        name="norm1",
    )(xp, xs, w.reshape(1, D_MODEL))


LINEAR_TILE_M = 1024
LINEAR_TILE_N = 1024


def _linear_kernel(*refs, n_lhs, n_aux, epilogue):
    lhs = refs[:n_lhs]
    rhs = refs[n_lhs:2 * n_lhs]
    aux = refs[2 * n_lhs:2 * n_lhs + n_aux]
    out = refs[2 * n_lhs + n_aux]
    accs = [jnp.dot(l[...], r[...], preferred_element_type=F32) for l, r in zip(lhs, rhs)]
    epilogue(accs, aux, out)


def _ep_cast(accs, aux, out):
    out[...] = accs[0].astype(out.dtype)


def _ep_silu(accs, aux, out):
    a = accs[0]
    out[...] = (a * _sigmoid(a)).astype(out.dtype)


def _ep_sigmoid(accs, aux, out):
    out[...] = _sigmoid(accs[0]).astype(out.dtype)


def _ep_head_norm(accs, aux, out):
    a = accs[0]
    w = aux[0][...]
    for j in range(a.shape[1] // HEAD_DIM):
        s = a[:, j * HEAD_DIM:(j + 1) * HEAD_DIM]
        ms = jnp.mean(s * s, axis=-1, keepdims=True)
        out[:, j * HEAD_DIM:(j + 1) * HEAD_DIM] = (s * lax.rsqrt(ms + NORM_EPS) * w).astype(out.dtype)


def _ep_gated_sum(accs, aux, out):
    out[...] = (aux[0][...].astype(F32) * accs[0] + aux[1][...].astype(F32) * accs[1]).astype(out.dtype)


def _linear(lhs_list, rhs_list, aux_list, aux_specs, epilogue, out_dtype, tm, tn, name):
    t, k = lhs_list[0].shape
    n = rhs_list[0].shape[1]
    tm = _pick_tile(t, tm)
    tn = _pick_tile(n, tn)
    in_specs = ([pl.BlockSpec((tm, k), lambda m, j: (m, 0)) for _ in lhs_list]
                + [pl.BlockSpec((k, tn), lambda m, j: (0, j)) for _ in rhs_list]
                + list(aux_specs(tm, tn)))
    return pl.pallas_call(
        functools.partial(_linear_kernel, n_lhs=len(lhs_list), n_aux=len(aux_list), epilogue=epilogue),
        grid=(t // tm, n // tn),
        in_specs=in_specs,
        out_specs=pl.BlockSpec((tm, tn), lambda m, j: (m, j)),
        out_shape=jax.ShapeDtypeStruct((t, n), out_dtype),
        compiler_params=_cparams(("parallel", "arbitrary"), 56),
        name=name,
    )(*lhs_list, *rhs_list, *aux_list)


def _no_aux(tm, tn):
    return []


def _attn_kernel(flags_ref, slope_ref, sink_ref, q_ref, k_ref, kp_ref, kn_ref, v_ref, vp_ref, vn_ref, o_ref,
                 *, n_sub):
    i = pl.program_id(0)
    g = pl.program_id(1)
    has_prev = flags_ref[0, i] == 0
    has_next = flags_ref[1, i] == 0

    qi = lax.broadcasted_iota(jnp.int32, (ATTN_BLOCK, 3 * ATTN_BLOCK), 0)
    kj = lax.broadcasted_iota(jnp.int32, (ATTN_BLOCK, 3 * ATTN_BLOCK), 1)
    dist = jnp.abs(ATTN_BLOCK + qi - kj)
    in_window = dist <= WINDOW
    dist_f = dist.astype(F32)
    is_prev_blk = kj < ATTN_BLOCK
    is_next_blk = kj >= 2 * ATTN_BLOCK

    for j in range(n_sub):
        rows = slice(j * ATTN_BLOCK, (j + 1) * ATTN_BLOCK)
        prev_rows = slice((j - 1) * ATTN_BLOCK, j * ATTN_BLOCK)
        next_rows = slice((j + 1) * ATTN_BLOCK, (j + 2) * ATTN_BLOCK)
        k_prev = kp_ref[...] if j == 0 else k_ref[prev_rows, :]
        v_prev = vp_ref[...] if j == 0 else v_ref[prev_rows, :]
        k_next = kn_ref[...] if j == n_sub - 1 else k_ref[next_rows, :]
        v_next = vn_ref[...] if j == n_sub - 1 else v_ref[next_rows, :]
        k_band = jnp.concatenate([k_prev, k_ref[rows, :], k_next], axis=0)
        v_band = jnp.concatenate([v_prev, v_ref[rows, :], v_next], axis=0)
        valid = in_window
        if j == 0:
            valid = valid & (has_prev | jnp.logical_not(is_prev_blk))
        if j == n_sub - 1:
            valid = valid & (has_next | jnp.logical_not(is_next_blk))
        q_stack = jnp.concatenate(
            [q_ref[rows, r * HEAD_DIM:(r + 1) * HEAD_DIM] for r in range(Q_PER_KV)], axis=0)
        s_all = lax.dot_general(q_stack, k_band, (((1,), (1,)), ((), ())), preferred_element_type=F32)
        for r in range(Q_PER_KV):
            head = g * Q_PER_KV + r
            s = s_all[r * ATTN_BLOCK:(r + 1) * ATTN_BLOCK, :] - slope_ref[head] * dist_f
            s = jnp.where(valid, s, MASK_VALUE)
            sink = sink_ref[head]
            m = jnp.maximum(jnp.max(s, axis=-1, keepdims=True), sink)
            p = jnp.exp(s - m)
            denom = jnp.sum(p, axis=-1, keepdims=True) + jnp.exp(sink - m)
            o = jnp.dot(p.astype(BF16), v_band, preferred_element_type=F32)
            o_ref[rows, r * HEAD_DIM:(r + 1) * HEAD_DIM] = (o / denom).astype(o_ref.dtype)


def _attention(q, k, v, sink, chunk_flags, tq):
    t = q.shape[0]
    n_chunks = t // tq
    n_blocks = t // ATTN_BLOCK
    sub = tq // ATTN_BLOCK
    slopes = jnp.asarray(2.0 ** (-8.0 * (np.arange(N_HEADS, dtype=np.float32) + 1.0) / N_HEADS), F32)
    gw = Q_PER_KV * HEAD_DIM

    def own(width):
        return pl.BlockSpec((tq, width), lambda i, g, *_: (i, g))

    prev = pl.BlockSpec((ATTN_BLOCK, HEAD_DIM), lambda i, g, *_: (jnp.maximum(i * sub - 1, 0), g))
    nxt = pl.BlockSpec((ATTN_BLOCK, HEAD_DIM), lambda i, g, *_: (jnp.minimum((i + 1) * sub, n_blocks - 1), g))
    grid_spec = pltpu.PrefetchScalarGridSpec(
        num_scalar_prefetch=3,
        grid=(n_chunks, N_KV_HEADS),
        in_specs=[own(gw), own(HEAD_DIM), prev, nxt, own(HEAD_DIM), prev, nxt],
        out_specs=own(gw),
    )
    return pl.pallas_call(
        functools.partial(_attn_kernel, n_sub=sub),
        grid_spec=grid_spec,
        out_shape=jax.ShapeDtypeStruct((t, N_HEADS * HEAD_DIM), BF16),
        compiler_params=_cparams(("parallel", "arbitrary"), 32),
        name="banded_attention",
    )(chunk_flags, slopes, sink.astype(F32), q, k, k, k, v, v, v)


CONV_HALO = BF16_SUBLANES
CONV_PAD = CONV_K // 2
CONV_TAP_ROWS = 8
CONV_COL_TILE = 512
HEAD_PAIR_WIDTH = 2 * SSD_HEAD_DIM


def _ssd_kernel(flags_ref, *rest, reverse):
    i = pl.program_id(0)
    c = pl.num_programs(0) - 1 - i if reverse else i
    seq_first = flags_ref[0, c] == 1
    seq_last = flags_ref[1, c] == 1
    L = SSD_CHUNK

    if reverse:
        xc_ref, dt_ref, dtb_ref, alog_ref, yf_ref, sz_ref, dskip_ref, normw_ref, out_ref, y_scr, state_scr = rest

        def xc(cols):
            return xc_ref[:, cols].astype(F32)
    else:
        (xbc_ref, xprev_ref, xnext_ref, dt_ref, convw_ref, convb_ref, dtb_ref, alog_ref, out_ref, xc_out_ref,
         ext_scr, xc_scr, y_scr, state_scr) = rest

        def xc(cols):
            return xc_scr[:, cols]

        zero_halo = jnp.zeros((CONV_HALO, C_XBC), BF16)
        ext_scr[0:CONV_HALO, :] = jnp.where(seq_first, zero_halo, xprev_ref[...])
        ext_scr[CONV_HALO:CONV_HALO + L, :] = xbc_ref[...]
        ext_scr[CONV_HALO + L:, :] = jnp.where(seq_last, zero_halo, xnext_ref[...])
        out_row = lax.broadcasted_iota(jnp.int32, (L, L + 2 * CONV_HALO), 0)
        src_row = lax.broadcasted_iota(jnp.int32, (L, L + 2 * CONV_HALO), 1)
        for ct in range(C_XBC // CONV_COL_TILE):
            cols = slice(ct * CONV_COL_TILE, (ct + 1) * CONV_COL_TILE)
            ext = ext_scr[:, cols]
            acc = convb_ref[:, cols] + xbc_ref[:, cols].astype(F32) * convw_ref[CONV_PAD:CONV_PAD + 1, cols]
            for j in range(CONV_K):
                if j == CONV_PAD:
                    continue
                shift = (src_row == out_row + (CONV_HALO - CONV_PAD + j)).astype(BF16)
                acc = acc + jnp.dot(shift, ext, preferred_element_type=F32) * convw_ref[j:j + 1, cols]
            conv = acc * _sigmoid(acc)
            xc_scr[:, cols] = conv
            xc_out_ref[:, cols] = conv.astype(xc_out_ref.dtype)

    col = lax.broadcasted_iota(jnp.int32, (1, LANES), 1)
    a_neg = jnp.where(col < 2 * SSD_HEADS, -jnp.exp(alog_ref[...]), 0.0)
    xdt = dt_ref[...] + dtb_ref[...]
    dt = jnp.maximum(xdt, 0.0) + jnp.log1p(jnp.exp(-jnp.abs(xdt)))
    a = dt * a_neg
    ri = lax.broadcasted_iota(jnp.int32, (L, L), 0)
    ci = lax.broadcasted_iota(jnp.int32, (L, L), 1)
    causal = (ri <= ci) if reverse else (ri >= ci)
    a_cum = jnp.dot(causal.astype(F32), a, precision=HIGHEST, preferred_element_type=F32)
    a_cum_t = a_cum.T
    dt_t = dt.T
    edge = 0 if reverse else L - 1
    a_total = a_cum[edge:edge + 1, :]
    w_state = dt * jnp.exp(a_total - a_cum)
    dir_off = SSD_HEADS if reverse else 0
    hr = lax.broadcasted_iota(jnp.int32, (LANES, D_INNER), 0)
    hc = lax.broadcasted_iota(jnp.int32, (LANES, D_INNER), 1)
    expand = (hr == dir_off + lax.shift_right_logical(hc, int(np.log2(SSD_HEAD_DIM)))).astype(F32)
    chunk_decay = jnp.dot(jnp.broadcast_to(jnp.exp(a_total), (8, LANES)), expand, precision=HIGHEST,
                          preferred_element_type=F32)[0:1, :]

    @pl.when(seq_last if reverse else seq_first)
    def _():
        state_scr[...] = jnp.zeros_like(state_scr)

    lo = lax.broadcasted_iota(jnp.int32, (L, LANES), 1) < SSD_HEAD_DIM
    for g in range(SSD_GROUPS):
        b_g = xc(slice(D_INNER + g * D_STATE, D_INNER + (g + 1) * D_STATE))
        c_g = xc(slice(D_INNER + (SSD_GROUPS + g) * D_STATE, D_INNER + (SSD_GROUPS + g + 1) * D_STATE))
        c_bf = c_g.astype(BF16)
        cb = lax.dot_general(c_bf, b_g.astype(BF16), (((1,), (1,)), ((), ())), preferred_element_type=F32)
        b_t = b_g.T.astype(BF16)
        state = state_scr[g]
        y_off = jnp.dot(c_bf, state.astype(BF16), preferred_element_type=F32)
        xw_parts = []
        for p in range(HEADS_PER_GROUP // 2):
            gcols = slice(g * GROUP_WIDTH + p * HEAD_PAIR_WIDTH, g * GROUP_WIDTH + (p + 1) * HEAD_PAIR_WIDTH)
            m_parts, e_parts, w_parts = [], [], []
            for hh in range(2):
                k = dir_off + g * HEADS_PER_GROUP + 2 * p + hh
                colb = jnp.broadcast_to(a_cum[:, k:k + 1], (L, L))
                rowb = jnp.broadcast_to(a_cum_t[k:k + 1, :], (L, L))
                dtrow = jnp.broadcast_to(dt_t[k:k + 1, :], (L, L))
                decay = jnp.exp(jnp.where(causal, colb - rowb, -jnp.inf))
                m_parts.append((cb * decay * dtrow).astype(BF16))
                e_parts.append(jnp.exp(colb))
                w_parts.append(jnp.broadcast_to(w_state[:, k:k + 1], (L, LANES)))
            x_pair = xc(gcols)
            rhs = jnp.concatenate([jnp.where(lo, x_pair, 0.0), jnp.where(lo, 0.0, x_pair)], axis=0).astype(BF16)
            y = jnp.dot(jnp.concatenate(m_parts, axis=1), rhs, preferred_element_type=F32)
            y = y + y_off[:, p * HEAD_PAIR_WIDTH:(p + 1) * HEAD_PAIR_WIDTH] * jnp.where(lo, e_parts[0], e_parts[1])
            y_scr[:, gcols] = y
            xw_parts.append((x_pair * jnp.where(lo, w_parts[0], w_parts[1])).astype(BF16))
        xw = jnp.concatenate(xw_parts, axis=1)
        state_scr[g] = (state * chunk_decay[:, g * GROUP_WIDTH:(g + 1) * GROUP_WIDTH]
                        + jnp.dot(b_t, xw, preferred_element_type=F32))

    if not reverse:
        out_ref[...] = y_scr[...].astype(out_ref.dtype)
    else:
        for g in range(SSD_GROUPS):
            cols = slice(g * GROUP_WIDTH, (g + 1) * GROUP_WIDTH)
            y = y_scr[:, cols] + yf_ref[:, cols].astype(F32) + xc(cols) * dskip_ref[:, cols]
            yg = y * sz_ref[:, cols].astype(F32)
            ms = jnp.mean(yg * yg, axis=-1, keepdims=True)
            out_ref[:, cols] = (yg * lax.rsqrt(ms + NORM_EPS) * normw_ref[:, cols]).astype(out_ref.dtype)


def _ssd(proj, seq_lengths, conv_w, conv_b, dt_bias, a_log, d_skip, ssd_norm_w):
    xbc, dt = proj['xbc'], proj['dt']
    t = xbc.shape[0]
    n_chunks = t // SSD_CHUNK
    halo_per_chunk = SSD_CHUNK // CONV_HALO
    n_halo_blocks = t // CONV_HALO
    flags = _sequence_flags(seq_lengths, SSD_CHUNK)
    pad = LANES - 2 * SSD_HEADS
    dtb = jnp.pad(dt_bias.astype(F32).reshape(1, -1), ((0, 0), (0, pad)))
    alog = jnp.pad(a_log.astype(F32).reshape(1, -1), ((0, 0), (0, pad)))
    scan_scratch = [pltpu.VMEM((SSD_CHUNK, D_INNER), F32), pltpu.VMEM((SSD_GROUPS, D_STATE, GROUP_WIDTH), F32)]

    def const(shape):
        return pl.BlockSpec(shape, lambda i, *_: (0, 0))

    def row(width):
        return pl.BlockSpec((SSD_CHUNK, width), lambda i, *_: (i, 0))

    def rev_row(width):
        return pl.BlockSpec((SSD_CHUNK, width), lambda i, *_: (n_chunks - 1 - i, 0))

    prev = pl.BlockSpec((CONV_HALO, C_XBC), lambda i, *_: (jnp.maximum(i * halo_per_chunk - 1, 0), 0))
    nxt = pl.BlockSpec((CONV_HALO, C_XBC),
                       lambda i, *_: (jnp.minimum((i + 1) * halo_per_chunk, n_halo_blocks - 1), 0))
    y_fwd, xc = pl.pallas_call(
        functools.partial(_ssd_kernel, reverse=False),
        grid_spec=pltpu.PrefetchScalarGridSpec(
            num_scalar_prefetch=1,
            grid=(n_chunks,),
            in_specs=[row(C_XBC), prev, nxt, row(LANES), const((CONV_TAP_ROWS, C_XBC)), const((1, C_XBC)),
                      const((1, LANES)), const((1, LANES))],
            out_specs=[row(D_INNER), row(C_XBC)],
            scratch_shapes=[pltpu.VMEM((SSD_CHUNK + 2 * CONV_HALO, C_XBC), BF16),
                            pltpu.VMEM((SSD_CHUNK, C_XBC), F32)] + scan_scratch,
        ),
        out_shape=[jax.ShapeDtypeStruct((t, D_INNER), BF16), jax.ShapeDtypeStruct((t, C_XBC), BF16)],
        compiler_params=_cparams(("arbitrary",), 40),
        name="ssd_fwd",
    )(flags, xbc, xbc, xbc, dt, jnp.pad(conv_w.astype(F32), ((0, CONV_TAP_ROWS - CONV_K), (0, 0))),
      conv_b.astype(F32).reshape(1, C_XBC), dtb, alog)

    d_lanes = jnp.repeat(d_skip.astype(F32), SSD_HEAD_DIM).reshape(1, D_INNER)
    return pl.pallas_call(
        functools.partial(_ssd_kernel, reverse=True),
        grid_spec=pltpu.PrefetchScalarGridSpec(
            num_scalar_prefetch=1,
            grid=(n_chunks,),
            in_specs=[rev_row(C_XBC), rev_row(LANES), const((1, LANES)), const((1, LANES)), rev_row(D_INNER),
                      rev_row(D_INNER), const((1, D_INNER)), const((1, D_INNER))],
            out_specs=rev_row(D_INNER),
            scratch_shapes=scan_scratch,
        ),
        out_shape=jax.ShapeDtypeStruct((t, D_INNER), BF16),
        compiler_params=_cparams(("arbitrary",), 40),
        name="ssd_bwd",
    )(flags, xc, dt, dtb, alog, y_fwd, proj['silu_z'], d_lanes, ssd_norm_w.astype(F32).reshape(1, D_INNER))


RANK_LANE = TOP_K


def _out_router_kernel(m_ref, w_ref, xp_ref, xs_ref, nw_ref, rw_ref, rb_ref, x1_ref, h2_ref, ti_ref, tw_ref,
                       cnt_ref, *, n_prompt_tiles):
    i = pl.program_id(0)
    acc = jnp.dot(m_ref[...], w_ref[...], preferred_element_type=F32)

    @pl.when(i == 0)
    def _():
        cnt_ref[...] = jnp.zeros_like(cnt_ref)

    def finish(x):
        x1 = x + acc
        x1_ref[...] = x1
        ms = jnp.mean(x1 * x1, axis=-1, keepdims=True)
        h2 = x1 * lax.rsqrt(ms + NORM_EPS) * nw_ref[...]
        h2_ref[...] = h2
        v = jnp.dot(h2.astype(BF16), rw_ref[...], preferred_element_type=F32) + rb_ref[...]
        lane = lax.broadcasted_iota(jnp.int32, v.shape, 1)
        vals, idxs = [], []
        for _ in range(TOP_K):
            top = jnp.max(v, axis=-1, keepdims=True)
            idx = jnp.min(jnp.where(v == top, lane, LANES), axis=-1, keepdims=True)
            vals.append(top)
            idxs.append(idx)
            v = jnp.where(lane == idx, -jnp.inf, v)
        es = [jnp.exp(val - vals[0]) for val in vals]
        total = es[0]
        for e in es[1:]:
            total = total + e
        n_rows = v.shape[0]
        chosen = jnp.zeros(v.shape, F32)
        for kk in range(TOP_K):
            chosen = chosen + (lane == idxs[kk]).astype(F32)
        ri = lax.broadcasted_iota(jnp.int32, (n_rows, n_rows), 0)
        ci = lax.broadcasted_iota(jnp.int32, (n_rows, n_rows), 1)
        before = jnp.dot((ci < ri).astype(BF16), chosen.astype(BF16), preferred_element_type=F32)
        before = before + cnt_ref[0:1, :]
        cnt_ref[...] = cnt_ref[...] + jnp.sum(chosen, axis=0, keepdims=True)
        ti = jnp.zeros(v.shape, jnp.int32)
        tw = jnp.zeros(v.shape, F32)
        for kk in range(TOP_K):
            rank = jnp.sum(jnp.where(lane == idxs[kk], before, 0.0), axis=-1, keepdims=True)
            ti = jnp.where(lane == kk, idxs[kk], ti)
            ti = jnp.where(lane == RANK_LANE + kk, rank.astype(jnp.int32), ti)
            tw = jnp.where(lane == kk, es[kk] / total, tw)
        ti_ref[...] = ti
        tw_ref[...] = tw

    finish(jnp.where(i < n_prompt_tiles, xp_ref[...], xs_ref[...]))


def _out_router(merged, w_out, xp, xs, norm2_w, router_w, router_b):
    tp, ts = xp.shape[0], xs.shape[0]
    t = tp + ts
    tile = _pick_tile(int(np.gcd(tp, ts)), 256)
    npt, nst = tp // tile, ts // tile
    p_spec, s_spec = _two_group_specs(tile, npt, nst, D_MODEL)
    rw = jnp.pad(router_w.astype(BF16), ((0, 0), (0, LANES - N_EXPERTS)))
    rb = jnp.pad(router_b.astype(F32).reshape(1, N_EXPERTS), ((0, 0), (0, LANES - N_EXPERTS)),
                 constant_values=-jnp.inf)

    def const(shape):
        return pl.BlockSpec(shape, lambda m: (0, 0))

    def row(width):
        return pl.BlockSpec((tile, width), lambda m: (m, 0))

    return pl.pallas_call(
        functools.partial(_out_router_kernel, n_prompt_tiles=npt),
        grid=(npt + nst,),
        in_specs=[row(D_MODEL), const((D_MODEL, D_MODEL)), p_spec, s_spec, const((1, D_MODEL)),
                  const((D_MODEL, LANES)), const((1, LANES))],
        out_specs=[row(D_MODEL), row(D_MODEL), row(LANES), row(LANES), const((8, LANES))],
        out_shape=[jax.ShapeDtypeStruct((t, D_MODEL), F32), jax.ShapeDtypeStruct((t, D_MODEL), F32),
                   jax.ShapeDtypeStruct((t, LANES), jnp.int32), jax.ShapeDtypeStruct((t, LANES), F32),
                   jax.ShapeDtypeStruct((8, LANES), F32)],
        compiler_params=_cparams(("arbitrary",), 48),
        name="out_proj_router",
    )(merged, w_out.astype(BF16), xp, xs, norm2_w.astype(F32).reshape(1, D_MODEL), rw, rb)


MOE_ROWS = 512
MOE_FF_TILE = 1024
MOE_GU_CHUNK = 128
MOE_D_CHUNK = 256
MOE_VMEM_MIB = 60


def _moe_kernel(be_ref, nb_ref, tok_hbm, h2_hbm, wgu_hbm, wd_hbm, bgu_ref, bd_ref, out_ref,
                idx_smem, xbuf, xb_scr, wgu_bf, wd_bf, stage_gu, stage_d, idx_sem, row_sem, w_sem):
    b = pl.program_id(0)
    n_used = nb_ref[0]
    expert = be_ref[b]
    expert_changed = jnp.logical_or(b == 0, expert != be_ref[jnp.maximum(b - 1, 0)])

    n_gu = D_MODEL // MOE_GU_CHUNK
    n_d = D_FF // MOE_D_CHUNK

    def weight_copy(k, slot):
        if k < n_gu:
            return pltpu.make_async_copy(wgu_hbm.at[expert, pl.ds(k * MOE_GU_CHUNK, MOE_GU_CHUNK), :],
                                         stage_gu.at[slot], w_sem.at[slot])
        k -= n_gu
        return pltpu.make_async_copy(wd_hbm.at[expert, pl.ds(k * MOE_D_CHUNK, MOE_D_CHUNK), :],
                                     stage_d.at[slot], w_sem.at[2 + slot])

    def load_expert():
        weight_copy(0, 0).start()
        weight_copy(1, 1).start()
        for k in range(n_gu + n_d):
            slot = k % 2
            weight_copy(k, slot).wait()
            if k < n_gu:
                wgu_bf[k * MOE_GU_CHUNK:(k + 1) * MOE_GU_CHUNK, :] = stage_gu[slot].astype(BF16)
            else:
                kd = k - n_gu
                wd_bf[kd * MOE_D_CHUNK:(kd + 1) * MOE_D_CHUNK, :] = stage_d[slot].astype(BF16)
            if k + 2 < n_gu + n_d:
                weight_copy(k + 2, slot).start()

    def idx_copy(block, slot):
        return pltpu.make_async_copy(tok_hbm.at[block], idx_smem.at[slot], idx_sem.at[slot])

    def issue_rows(slot):
        def body(r, carry):
            tok = idx_smem[slot, r]
            pltpu.make_async_copy(h2_hbm.at[pl.ds(tok, 1), :], xbuf.at[slot, pl.ds(r, 1), :],
                                  row_sem.at[slot]).start()
            return carry
        lax.fori_loop(0, MOE_ROWS, body, 0, unroll=True)

    def wait_rows(slot):
        pltpu.make_async_copy(h2_hbm.at[pl.ds(0, MOE_ROWS), :], xbuf.at[slot], row_sem.at[slot]).wait()

    @pl.when(b < n_used)
    def _():
        slot = lax.rem(b, 2)
        nslot = 1 - slot

        @pl.when(b == 0)
        def _():
            idx_copy(0, 0).start()
            idx_copy(0, 0).wait()
            issue_rows(0)

            @pl.when(n_used > 1)
            def _():
                idx_copy(1, 1).start()

        @pl.when(b + 1 < n_used)
        def _():
            idx_copy(b + 1, nslot).wait()
            for s in range(2):
                @pl.when(nslot == s)
                def _():
                    issue_rows(s)

            @pl.when(b + 2 < n_used)
            def _():
                idx_copy(b + 2, slot).start()

        @pl.when(expert_changed)
        def _():
            load_expert()

        wait_rows(slot)
        xb_scr[...] = xbuf[slot].astype(BF16)

        for j in range(D_FF // MOE_FF_TILE):
            cols = slice(j * MOE_FF_TILE, (j + 1) * MOE_FF_TILE)
            up_cols = slice(D_FF + j * MOE_FF_TILE, D_FF + (j + 1) * MOE_FF_TILE)
            xb = xb_scr[...]
            gate = jnp.dot(xb, wgu_bf[:, cols], preferred_element_type=F32) + bgu_ref[0, :, cols]
            up = jnp.dot(xb, wgu_bf[:, up_cols], preferred_element_type=F32) + bgu_ref[0, :, up_cols]
            gate = jnp.minimum(gate, SWIGLU_LIMIT)
            up = jnp.clip(up, -SWIGLU_LIMIT, SWIGLU_LIMIT)
            act = (gate * _sigmoid(SWIGLU_ALPHA * gate) * (up + 1.0)).astype(BF16)
            part = jnp.dot(act, wd_bf[cols, :], preferred_element_type=F32)
            out_ref[...] = (bd_ref[0] if j == 0 else out_ref[...]) + part

    @pl.when(b >= n_used)
    def _():
        out_ref[...] = jnp.zeros_like(out_ref)


def _moe(h2, tok_sorted, block_expert, n_used, w_gate_up, b_gate_up, w_down, b_down):
    n_blocks = tok_sorted.shape[0]
    bgu = b_gate_up.astype(F32).reshape(N_EXPERTS, 1, 2 * D_FF)
    bd = b_down.astype(F32).reshape(N_EXPERTS, 1, D_MODEL)
    grid_spec = pltpu.PrefetchScalarGridSpec(
        num_scalar_prefetch=2,
        grid=(n_blocks,),
        in_specs=[
            pl.BlockSpec(memory_space=pl.ANY),
            pl.BlockSpec(memory_space=pl.ANY),
            pl.BlockSpec(memory_space=pl.ANY),
            pl.BlockSpec(memory_space=pl.ANY),
            pl.BlockSpec((1, 1, 2 * D_FF), lambda b, be, nb: (be[b], 0, 0)),
            pl.BlockSpec((1, 1, D_MODEL), lambda b, be, nb: (be[b], 0, 0)),
        ],
        out_specs=pl.BlockSpec((MOE_ROWS, D_MODEL), lambda b, be, nb: (b, 0)),
        scratch_shapes=[
            pltpu.SMEM((2, MOE_ROWS), jnp.int32),
            pltpu.VMEM((2, MOE_ROWS, D_MODEL), F32),
            pltpu.VMEM((MOE_ROWS, D_MODEL), BF16),
            pltpu.VMEM((D_MODEL, 2 * D_FF), BF16),
            pltpu.VMEM((D_FF, D_MODEL), BF16),
            pltpu.VMEM((2, MOE_GU_CHUNK, 2 * D_FF), F32),
            pltpu.VMEM((2, MOE_D_CHUNK, D_MODEL), F32),
            pltpu.SemaphoreType.DMA((2,)),
            pltpu.SemaphoreType.DMA((2,)),
            pltpu.SemaphoreType.DMA((4,)),
        ],
    )
    return pl.pallas_call(
        _moe_kernel,
        grid_spec=grid_spec,
        out_shape=jax.ShapeDtypeStruct((n_blocks * MOE_ROWS, D_MODEL), F32),
        compiler_params=_cparams(("arbitrary",), MOE_VMEM_MIB),
        name="moe_experts",
    )(block_expert, n_used, tok_sorted, h2, w_gate_up.astype(F32), w_down.astype(F32), bgu, bd)


def _route(top_idx, rank, counts, n_blocks):
    t = top_idx.shape[0]
    flat_e = top_idx.reshape(-1)
    rank = rank.reshape(-1)
    blocks_e = (counts + MOE_ROWS - 1) // MOE_ROWS
    blocks_end = jnp.cumsum(blocks_e)
    blocks_start = blocks_end - blocks_e
    dest = (blocks_start[flat_e] * MOE_ROWS + rank).astype(jnp.int32)
    n_used = blocks_end[-1].astype(jnp.int32)
    flat_tok = jnp.arange(t * TOP_K, dtype=jnp.int32) // TOP_K
    tok_sorted = jnp.zeros((n_blocks * MOE_ROWS,), jnp.int32).at[dest].set(flat_tok, unique_indices=True)
    bidx = jnp.arange(n_blocks, dtype=jnp.int32)
    be = jnp.minimum(jnp.sum((blocks_end[None, :] <= bidx[:, None]).astype(jnp.int32), axis=1), N_EXPERTS - 1)
    be = jnp.where(bidx < n_used, be, be[jnp.maximum(n_used - 1, 0)])
    return tok_sorted.reshape(n_blocks, MOE_ROWS), be, n_used.reshape(1), dest


COMBINE_ROWS = 128


def _combine_kernel(pos_hbm, rows_hbm, x1_ref, tw_ref, yp_ref, ys_ref, pos_smem, gbuf, pos_sem, row_sem,
                    *, n_prompt_tiles):
    i = pl.program_id(0)
    n = pl.num_programs(0)
    slot = lax.rem(i, 2)
    nslot = 1 - slot

    def pos_copy(tile, s):
        return pltpu.make_async_copy(pos_hbm.at[tile], pos_smem.at[s], pos_sem.at[s])

    def issue_rows(s):
        def body(r, carry):
            for kk in range(TOP_K):
                src = pos_smem[s, r * TOP_K + kk]
                pltpu.make_async_copy(rows_hbm.at[pl.ds(src, 1), :], gbuf.at[s, kk, pl.ds(r, 1), :],
                                      row_sem.at[s]).start()
            return carry
        lax.fori_loop(0, COMBINE_ROWS, body, 0, unroll=True)

    def wait_rows(s):
        for kk in range(TOP_K):
            pltpu.make_async_copy(rows_hbm.at[pl.ds(0, COMBINE_ROWS), :], gbuf.at[s, kk], row_sem.at[s]).wait()

    @pl.when(i == 0)
    def _():
        pos_copy(0, 0).start()
        pos_copy(0, 0).wait()
        issue_rows(0)

        @pl.when(n > 1)
        def _():
            pos_copy(1, 1).start()

    @pl.when(i + 1 < n)
    def _():
        pos_copy(i + 1, nslot).wait()
        for s in range(2):
            @pl.when(nslot == s)
            def _():
                issue_rows(s)

        @pl.when(i + 2 < n)
        def _():
            pos_copy(i + 2, slot).start()

    wait_rows(slot)
    tw = tw_ref[...]
    y = x1_ref[...]
    for kk in range(TOP_K):
        y = y + tw[:, kk:kk + 1] * gbuf[slot, kk]

    @pl.when(i < n_prompt_tiles)
    def _():
        yp_ref[...] = y

    @pl.when(i >= n_prompt_tiles)
    def _():
        ys_ref[...] = y


def _combine(x1, expert_rows, dest, top_w, tp, ts):
    t = tp + ts
    tile = _pick_tile(int(np.gcd(tp, ts)), COMBINE_ROWS)
    assert tile == COMBINE_ROWS
    npt, nst = tp // tile, ts // tile
    p_spec, s_spec = _two_group_specs(tile, npt, nst, D_MODEL)
    return pl.pallas_call(
        functools.partial(_combine_kernel, n_prompt_tiles=npt),
        grid=(npt + nst,),
        in_specs=[pl.BlockSpec(memory_space=pl.ANY), pl.BlockSpec(memory_space=pl.ANY),
                  pl.BlockSpec((tile, D_MODEL), lambda m: (m, 0)), pl.BlockSpec((tile, LANES), lambda m: (m, 0))],
        out_specs=[p_spec, s_spec],
        out_shape=[jax.ShapeDtypeStruct((tp, D_MODEL), F32), jax.ShapeDtypeStruct((ts, D_MODEL), F32)],
        scratch_shapes=[
            pltpu.SMEM((2, COMBINE_ROWS * TOP_K), jnp.int32),
            pltpu.VMEM((2, TOP_K, COMBINE_ROWS, D_MODEL), F32),
            pltpu.SemaphoreType.DMA((2,)),
            pltpu.SemaphoreType.DMA((2,)),
        ],
        compiler_params=_cparams(("arbitrary",), 32),
        name="moe_combine",
    )(dest.reshape(t // tile, tile * TOP_K), expert_rows, x1, top_w)


def _sequence_flags(seq_lengths, unit):
    first, last = [], []
    for length in seq_lengths:
        n = length // unit
        first += [1] + [0] * (n - 1)
        last += [0] * (n - 1) + [1]
    return jnp.asarray(np.array([first, last], np.int32))


def _in_projections(h, w_in, q_norm_w, k_norm_w):
    w = w_in.astype(BF16)
    offs = np.cumsum([0, N_HEADS * HEAD_DIM, N_KV_HEADS * HEAD_DIM, N_KV_HEADS * HEAD_DIM, D_INNER, C_XBC,
                      2 * SSD_HEADS, D_MODEL, D_MODEL])
    seg = [w[:, offs[i]:offs[i + 1]] for i in range(8)]
    w_dt = jnp.pad(seg[5], ((0, 0), (0, LANES - 2 * SSD_HEADS)))

    def head_w_spec(tm, tn):
        return [pl.BlockSpec((1, HEAD_DIM), lambda m, j: (0, 0))]

    qw = (q_norm_w.astype(F32) * (1.0 / np.sqrt(HEAD_DIM))).reshape(1, HEAD_DIM)
    kw = k_norm_w.astype(F32).reshape(1, HEAD_DIM)
    out = {}
    tm, tn = LINEAR_TILE_M, LINEAR_TILE_N
    out['q'] = _linear([h], [seg[0]], [qw], head_w_spec, _ep_head_norm, BF16, tm, tn, "proj_q")
    out['k'] = _linear([h], [seg[1]], [kw], head_w_spec, _ep_head_norm, BF16, tm, tn, "proj_k")
    out['v'] = _linear([h], [seg[2]], [], _no_aux, _ep_cast, BF16, tm, tn, "proj_v")
    out['silu_z'] = _linear([h], [seg[3]], [], _no_aux, _ep_silu, BF16, tm, tn, "proj_z")
    out['xbc'] = _linear([h], [seg[4]], [], _no_aux, _ep_cast, BF16, tm, tn, "proj_xbc")
    out['dt'] = _linear([h], [w_dt], [], _no_aux, _ep_cast, F32, tm, tn, "proj_dt")
    out['gate_attn'] = _linear([h], [seg[6]], [], _no_aux, _ep_sigmoid, BF16, tm, tn, "proj_gate_attn")
    out['gate_ssd'] = _linear([h], [seg[7]], [], _no_aux, _ep_sigmoid, BF16, tm, tn, "proj_gate_ssd")
    return out


def _pre_attention(x_prompt, x_sample, p):
    xp = x_prompt.reshape(-1, D_MODEL)
    xs = x_sample.reshape(-1, D_MODEL)
    seq_lengths = [x_prompt.shape[1]] * x_prompt.shape[0] + [x_sample.shape[1]] * x_sample.shape[0]
    h = _norm1(xp, xs, p['norm1_w'][0])
    proj = _in_projections(h, p['w_in'][0], p['q_norm_w'][0], p['k_norm_w'][0])
    tq = _pick_tile(int(np.gcd.reduce(seq_lengths)), 512)
    flags = _sequence_flags(seq_lengths, tq)
    proj['attn'] = _attention(proj['q'], proj['k'], proj['v'], p['attn_sink'][0], flags, tq)
    proj['seq_lengths'] = seq_lengths
    if 'conv_w' in p:
        proj['ssd'] = _ssd(proj, seq_lengths, p['conv_w'][0], p['conv_b'][0], p['dt_bias'][0], p['a_log'][0],
                           p['d_skip'][0], p['ssd_norm_w'][0])
    return proj


def kernel(x_prompt, x_sample, norm1_w, w_in, q_norm_w, k_norm_w, attn_sink, conv_w, conv_b, dt_bias, a_log, d_skip, ssd_norm_w, w_attn_proj, w_ssd_proj, w_out, norm2_w, router_w, router_b, w_gate_up, b_gate_up, w_down, b_down):
    assert norm1_w.shape[0] == 1, "single-layer block"
    p = dict(norm1_w=norm1_w, w_in=w_in, q_norm_w=q_norm_w, k_norm_w=k_norm_w, attn_sink=attn_sink,
             conv_w=conv_w, conv_b=conv_b, dt_bias=dt_bias, a_log=a_log, d_skip=d_skip, ssd_norm_w=ssd_norm_w)
    xp = x_prompt.reshape(-1, D_MODEL)
    xs = x_sample.reshape(-1, D_MODEL)
    tp, ts = xp.shape[0], xs.shape[0]
    t = tp + ts
    pre = _pre_attention(x_prompt, x_sample, p)

    def gate_specs(tm, tn):
        return [pl.BlockSpec((tm, tn), lambda m, j: (m, j))] * 2

    merged = _linear([pre['attn'], pre['ssd']], [w_attn_proj[0].astype(BF16), w_ssd_proj[0].astype(BF16)],
                     [pre['gate_attn'], pre['gate_ssd']], gate_specs, _ep_gated_sum, BF16, LINEAR_TILE_M,
                     LINEAR_TILE_N, "branch_merge")
    x1, h2, top_idx, top_w, counts = _out_router(merged, w_out[0], xp, xs, norm2_w[0], router_w[0], router_b[0])

    n_blocks = -(-(t * TOP_K) // MOE_ROWS) + N_EXPERTS
    tok_sorted, block_expert, n_used, dest = _route(top_idx[:, :TOP_K], top_idx[:, RANK_LANE:RANK_LANE + TOP_K],
                                                    counts[0, :N_EXPERTS].astype(jnp.int32), n_blocks)
    expert_rows = _moe(h2, tok_sorted, block_expert, n_used, w_gate_up[0], b_gate_up[0], w_down[0], b_down[0])
    yp, ys = _combine(x1, expert_rows, dest, top_w, tp, ts)
    return yp.reshape(x_prompt.shape), ys.reshape(x_sample.shape)
```

```python
import functools

import numpy as np
import jax
import jax.numpy as jnp
from jax import lax
from jax.experimental import pallas as pl
from jax.experimental.pallas import tpu as pltpu

F32 = jnp.float32
BF16 = jnp.bfloat16
HIGHEST = lax.Precision.HIGHEST

D_MODEL = 2048
N_HEADS = 16
N_KV_HEADS = 4
Q_PER_KV = N_HEADS // N_KV_HEADS
HEAD_DIM = 128
WINDOW = 128
ATTN_BLOCK = 128
D_INNER = 2048
SSD_HEAD_DIM = 64
SSD_HEADS = D_INNER // SSD_HEAD_DIM
SSD_GROUPS = 4
HEADS_PER_GROUP = SSD_HEADS // SSD_GROUPS
GROUP_WIDTH = D_INNER // SSD_GROUPS
D_STATE = 128
CONV_K = 5
SSD_CHUNK = 128
C_XBC = D_INNER + 2 * SSD_GROUPS * D_STATE
N_EXPERTS = 32
TOP_K = 4
D_FF = D_MODEL
SWIGLU_LIMIT = 7.0
SWIGLU_ALPHA = 1.702
NORM_EPS = 1e-6
MASK_VALUE = -1e30
LOG2_E = float(np.log2(np.e))

LANES = 128
BF16_SUBLANES = 16
MIB = 1 << 20


def _cparams(semantics, vmem_mib):
    return pltpu.CompilerParams(dimension_semantics=semantics, vmem_limit_bytes=vmem_mib * MIB)


def _sigmoid(x):
    return 1.0 / (1.0 + jnp.exp(-x))


def _pick_tile(total, preferred):
    t = min(total, preferred)
    while total % t:
        t //= 2
    return t


def _norm1_kernel(xp_ref, xs_ref, w_ref, o_ref, *, n_prompt_tiles):
    m = pl.program_id(0)

    def body(x_ref):
        x = x_ref[...]
        ms = jnp.mean(x * x, axis=-1, keepdims=True)
        o_ref[...] = (x * lax.rsqrt(ms + NORM_EPS) * w_ref[...]).astype(BF16)

    @pl.when(m < n_prompt_tiles)
    def _():
        body(xp_ref)

    @pl.when(m >= n_prompt_tiles)
    def _():
        body(xs_ref)


def _two_group_specs(tile, n_prompt_tiles, n_sample_tiles, width):
    p_spec = pl.BlockSpec((tile, width), lambda m: (jnp.minimum(m, n_prompt_tiles - 1), 0))
    s_spec = pl.BlockSpec((tile, width), lambda m: (jnp.maximum(m - n_prompt_tiles, 0), 0))
    return p_spec, s_spec


def _norm1(xp, xs, w):
    tp, ts = xp.shape[0], xs.shape[0]
    tile = _pick_tile(int(np.gcd(tp, ts)), 512)
    npt, nst = tp // tile, ts // tile
    p_spec, s_spec = _two_group_specs(tile, npt, nst, D_MODEL)
    return pl.pallas_call(
        functools.partial(_norm1_kernel, n_prompt_tiles=npt),
        grid=(npt + nst,),
        in_specs=[p_spec, s_spec, pl.BlockSpec((1, D_MODEL), lambda m: (0, 0))],
        out_specs=pl.BlockSpec((tile, D_MODEL), lambda m: (m, 0)),
        out_shape=jax.ShapeDtypeStruct((tp + ts, D_MODEL), BF16),
        compiler_params=_cparams(("parallel",), 40),
        name="norm1",
    )(xp, xs, w.reshape(1, D_MODEL))


LINEAR_TILE_M = 1024
LINEAR_TILE_N = 1024


def _linear_kernel(*refs, n_lhs, n_aux, epilogue):
    lhs = refs[:n_lhs]
    rhs = refs[n_lhs:2 * n_lhs]
    aux = refs[2 * n_lhs:2 * n_lhs + n_aux]
    out = refs[2 * n_lhs + n_aux]
    accs = [jnp.dot(l[...], r[...], preferred_element_type=F32) for l, r in zip(lhs, rhs)]
    epilogue(accs, aux, out)


def _ep_cast(accs, aux, out):
    out[...] = accs[0].astype(out.dtype)


def _ep_silu(accs, aux, out):
    a = accs[0]
    out[...] = (a * _sigmoid(a)).astype(out.dtype)


def _ep_sigmoid(accs, aux, out):
    out[...] = _sigmoid(accs[0]).astype(out.dtype)


def _ep_head_norm(accs, aux, out):
    a = accs[0]
    w = aux[0][...]
    for j in range(a.shape[1] // HEAD_DIM):
        s = a[:, j * HEAD_DIM:(j + 1) * HEAD_DIM]
        ms = jnp.mean(s * s, axis=-1, keepdims=True)
        out[:, j * HEAD_DIM:(j + 1) * HEAD_DIM] = (s * lax.rsqrt(ms + NORM_EPS) * w).astype(out.dtype)


def _ep_gated_sum(accs, aux, out):
    out[...] = (aux[0][...].astype(F32) * accs[0] + aux[1][...].astype(F32) * accs[1]).astype(out.dtype)


def _linear(lhs_list, rhs_list, aux_list, aux_specs, epilogue, out_dtype, tm, tn, name):
    t, k = lhs_list[0].shape
    n = rhs_list[0].shape[1]
    tm = _pick_tile(t, tm)
    tn = _pick_tile(n, tn)
    in_specs = ([pl.BlockSpec((tm, k), lambda m, j: (m, 0)) for _ in lhs_list]
                + [pl.BlockSpec((k, tn), lambda m, j: (0, j)) for _ in rhs_list]
                + list(aux_specs(tm, tn)))
    return pl.pallas_call(
        functools.partial(_linear_kernel, n_lhs=len(lhs_list), n_aux=len(aux_list), epilogue=epilogue),
        grid=(t // tm, n // tn),
        in_specs=in_specs,
        out_specs=pl.BlockSpec((tm, tn), lambda m, j: (m, j)),
        out_shape=jax.ShapeDtypeStruct((t, n), out_dtype),
        compiler_params=_cparams(("parallel", "arbitrary"), 56),
        name=name,
    )(*lhs_list, *rhs_list, *aux_list)


def _no_aux(tm, tn):
    return []


def _attn_kernel(flags_ref, slope_ref, sink_ref, q_ref, k_ref, kp_ref, kn_ref, v_ref, vp_ref, vn_ref, o_ref,
                 *, n_sub):
    i = pl.program_id(0)
    g = pl.program_id(1)
    has_prev = flags_ref[0, i] == 0
    has_next = flags_ref[1, i] == 0

    qi = lax.broadcasted_iota(jnp.int32, (ATTN_BLOCK, 3 * ATTN_BLOCK), 0)
    kj = lax.broadcasted_iota(jnp.int32, (ATTN_BLOCK, 3 * ATTN_BLOCK), 1)
    dist = jnp.abs(ATTN_BLOCK + qi - kj)
    in_window = dist <= WINDOW
    dist_f = dist.astype(F32)
    is_prev_blk = kj < ATTN_BLOCK
    is_next_blk = kj >= 2 * ATTN_BLOCK

    for j in range(n_sub):
        rows = slice(j * ATTN_BLOCK, (j + 1) * ATTN_BLOCK)
        prev_rows = slice((j - 1) * ATTN_BLOCK, j * ATTN_BLOCK)
        next_rows = slice((j + 1) * ATTN_BLOCK, (j + 2) * ATTN_BLOCK)
        k_prev = kp_ref[...] if j == 0 else k_ref[prev_rows, :]
        v_prev = vp_ref[...] if j == 0 else v_ref[prev_rows, :]
        k_next = kn_ref[...] if j == n_sub - 1 else k_ref[next_rows, :]
        v_next = vn_ref[...] if j == n_sub - 1 else v_ref[next_rows, :]
        k_band = jnp.concatenate([k_prev, k_ref[rows, :], k_next], axis=0)
        v_band = jnp.concatenate([v_prev, v_ref[rows, :], v_next], axis=0)
        valid = in_window
        if j == 0:
            valid = valid & (has_prev | jnp.logical_not(is_prev_blk))
        if j == n_sub - 1:
            valid = valid & (has_next | jnp.logical_not(is_next_blk))
        q_stack = jnp.concatenate(
            [q_ref[rows, r * HEAD_DIM:(r + 1) * HEAD_DIM] for r in range(Q_PER_KV)], axis=0)
        s_all = lax.dot_general(q_stack, k_band, (((1,), (1,)), ((), ())), preferred_element_type=F32)
        for r in range(Q_PER_KV):
            head = g * Q_PER_KV + r
            s = s_all[r * ATTN_BLOCK:(r + 1) * ATTN_BLOCK, :] - slope_ref[head] * dist_f
            s = jnp.where(valid, s, MASK_VALUE)
            sink = sink_ref[head]
            m = jnp.maximum(jnp.max(s, axis=-1, keepdims=True), sink)
            p = jnp.exp2(s - m)
            denom = jnp.sum(p, axis=-1, keepdims=True) + jnp.exp2(sink - m)
            o = jnp.dot(p.astype(BF16), v_band, preferred_element_type=F32)
            o_ref[rows, r * HEAD_DIM:(r + 1) * HEAD_DIM] = (o / denom).astype(o_ref.dtype)


def _attention(q, k, v, sink, chunk_flags, tq):
    t = q.shape[0]
    n_chunks = t // tq
    n_blocks = t // ATTN_BLOCK
    sub = tq // ATTN_BLOCK
    slopes = jnp.asarray(LOG2_E * 2.0 ** (-8.0 * (np.arange(N_HEADS, dtype=np.float32) + 1.0) / N_HEADS), F32)
    gw = Q_PER_KV * HEAD_DIM

    def own(width):
        return pl.BlockSpec((tq, width), lambda i, g, *_: (i, g))

    prev = pl.BlockSpec((ATTN_BLOCK, HEAD_DIM), lambda i, g, *_: (jnp.maximum(i * sub - 1, 0), g))
    nxt = pl.BlockSpec((ATTN_BLOCK, HEAD_DIM), lambda i, g, *_: (jnp.minimum((i + 1) * sub, n_blocks - 1), g))
    grid_spec = pltpu.PrefetchScalarGridSpec(
        num_scalar_prefetch=3,
        grid=(n_chunks, N_KV_HEADS),
        in_specs=[own(gw), own(HEAD_DIM), prev, nxt, own(HEAD_DIM), prev, nxt],
        out_specs=own(gw),
    )
    return pl.pallas_call(
        functools.partial(_attn_kernel, n_sub=sub),
        grid_spec=grid_spec,
        out_shape=jax.ShapeDtypeStruct((t, N_HEADS * HEAD_DIM), BF16),
        compiler_params=_cparams(("parallel", "arbitrary"), 32),
        name="banded_attention",
    )(chunk_flags, slopes, sink.astype(F32) * LOG2_E, q, k, k, k, v, v, v)


CONV_HALO = BF16_SUBLANES
CONV_PAD = CONV_K // 2
CONV_TAP_ROWS = 8
CONV_COL_TILE = 512
HEAD_PAIR_WIDTH = 2 * SSD_HEAD_DIM


def _ssd_kernel(flags_ref, *rest, reverse):
    i = pl.program_id(0)
    c = pl.num_programs(0) - 1 - i if reverse else i
    seq_first = flags_ref[0, c] == 1
    seq_last = flags_ref[1, c] == 1
    L = SSD_CHUNK

    if reverse:
        xc_ref, dt_ref, dtb_ref, alog_ref, yf_ref, sz_ref, dskip_ref, normw_ref, out_ref, y_scr, state_scr = rest

        def xc(cols):
            return xc_ref[:, cols].astype(F32)
    else:
        (xbc_ref, xprev_ref, xnext_ref, dt_ref, convw_ref, convb_ref, dtb_ref, alog_ref, out_ref, xc_out_ref,
         ext_scr, xc_scr, y_scr, state_scr) = rest

        def xc(cols):
            return xc_scr[:, cols]

        zero_halo = jnp.zeros((CONV_HALO, C_XBC), BF16)
        ext_scr[0:CONV_HALO, :] = jnp.where(seq_first, zero_halo, xprev_ref[...])
        ext_scr[CONV_HALO:CONV_HALO + L, :] = xbc_ref[...]
        ext_scr[CONV_HALO + L:, :] = jnp.where(seq_last, zero_halo, xnext_ref[...])
        out_row = lax.broadcasted_iota(jnp.int32, (L, L + 2 * CONV_HALO), 0)
        src_row = lax.broadcasted_iota(jnp.int32, (L, L + 2 * CONV_HALO), 1)
        for ct in range(C_XBC // CONV_COL_TILE):
            cols = slice(ct * CONV_COL_TILE, (ct + 1) * CONV_COL_TILE)
            ext = ext_scr[:, cols]
            acc = convb_ref[:, cols] + xbc_ref[:, cols].astype(F32) * convw_ref[CONV_PAD:CONV_PAD + 1, cols]
            for j in range(CONV_K):
                if j == CONV_PAD:
                    continue
                shift = (src_row == out_row + (CONV_HALO - CONV_PAD + j)).astype(BF16)
                acc = acc + jnp.dot(shift, ext, preferred_element_type=F32) * convw_ref[j:j + 1, cols]
            conv = acc * _sigmoid(acc)
            xc_scr[:, cols] = conv
            xc_out_ref[:, cols] = conv.astype(xc_out_ref.dtype)

    col = lax.broadcasted_iota(jnp.int32, (1, LANES), 1)
    a_neg = jnp.where(col < 2 * SSD_HEADS, -jnp.exp(alog_ref[...]), 0.0)
    xdt = dt_ref[...] + dtb_ref[...]
    dt = jnp.maximum(xdt, 0.0) + jnp.log1p(jnp.exp(-jnp.abs(xdt)))
    a = dt * (a_neg * LOG2_E)
    ri = lax.broadcasted_iota(jnp.int32, (L, L), 0)
    ci = lax.broadcasted_iota(jnp.int32, (L, L), 1)
    causal = (ri <= ci) if reverse else (ri >= ci)
    a_cum = jnp.dot(causal.astype(F32), a, precision=HIGHEST, preferred_element_type=F32)
    a_cum_t = a_cum.T
    dt_t = dt.T
    edge = 0 if reverse else L - 1
    a_total = a_cum[edge:edge + 1, :]
    w_state = dt * jnp.exp2(a_total - a_cum)
    dir_off = SSD_HEADS if reverse else 0
    hr = lax.broadcasted_iota(jnp.int32, (LANES, D_INNER), 0)
    hc = lax.broadcasted_iota(jnp.int32, (LANES, D_INNER), 1)
    expand = (hr == dir_off + lax.shift_right_logical(hc, int(np.log2(SSD_HEAD_DIM)))).astype(F32)
    chunk_decay = jnp.dot(jnp.broadcast_to(jnp.exp2(a_total), (8, LANES)), expand, precision=HIGHEST,
                          preferred_element_type=F32)[0:1, :]

    @pl.when(seq_last if reverse else seq_first)
    def _():
        state_scr[...] = jnp.zeros_like(state_scr)

    lo = lax.broadcasted_iota(jnp.int32, (L, LANES), 1) < SSD_HEAD_DIM
    for g in range(SSD_GROUPS):
        b_g = xc(slice(D_INNER + g * D_STATE, D_INNER + (g + 1) * D_STATE))
        c_g = xc(slice(D_INNER + (SSD_GROUPS + g) * D_STATE, D_INNER + (SSD_GROUPS + g + 1) * D_STATE))
        c_bf = c_g.astype(BF16)
        cb = lax.dot_general(c_bf, b_g.astype(BF16), (((1,), (1,)), ((), ())), preferred_element_type=F32)
        b_t = b_g.T.astype(BF16)
        state = state_scr[g]
        y_off = jnp.dot(c_bf, state.astype(BF16), preferred_element_type=F32)
        xw_parts = []
        for p in range(HEADS_PER_GROUP // 2):
            gcols = slice(g * GROUP_WIDTH + p * HEAD_PAIR_WIDTH, g * GROUP_WIDTH + (p + 1) * HEAD_PAIR_WIDTH)
            m_parts, e_parts, w_parts = [], [], []
            for hh in range(2):
                k = dir_off + g * HEADS_PER_GROUP + 2 * p + hh
                colb = jnp.broadcast_to(a_cum[:, k:k + 1], (L, L))
                rowb = jnp.broadcast_to(a_cum_t[k:k + 1, :], (L, L))
                dtrow = jnp.broadcast_to(dt_t[k:k + 1, :], (L, L))
                decay = jnp.exp2(jnp.where(causal, colb - rowb, -jnp.inf))
                m_parts.append((cb * decay * dtrow).astype(BF16))
                e_parts.append(jnp.exp2(colb))
                w_parts.append(jnp.broadcast_to(w_state[:, k:k + 1], (L, LANES)))
            x_pair = xc(gcols)
            rhs = jnp.concatenate([jnp.where(lo, x_pair, 0.0), jnp.where(lo, 0.0, x_pair)], axis=0).astype(BF16)
            y = jnp.dot(jnp.concatenate(m_parts, axis=1), rhs, preferred_element_type=F32)
            y = y + y_off[:, p * HEAD_PAIR_WIDTH:(p + 1) * HEAD_PAIR_WIDTH] * jnp.where(lo, e_parts[0], e_parts[1])
            y_scr[:, gcols] = y
            xw_parts.append((x_pair * jnp.where(lo, w_parts[0], w_parts[1])).astype(BF16))
        xw = jnp.concatenate(xw_parts, axis=1)
        state_scr[g] = (state * chunk_decay[:, g * GROUP_WIDTH:(g + 1) * GROUP_WIDTH]
                        + jnp.dot(b_t, xw, preferred_element_type=F32))

    if not reverse:
        out_ref[...] = y_scr[...].astype(out_ref.dtype)
    else:
        for g in range(SSD_GROUPS):
            cols = slice(g * GROUP_WIDTH, (g + 1) * GROUP_WIDTH)
            y = y_scr[:, cols] + yf_ref[:, cols].astype(F32) + xc(cols) * dskip_ref[:, cols]
            yg = y * sz_ref[:, cols].astype(F32)
            ms = jnp.mean(yg * yg, axis=-1, keepdims=True)
            out_ref[:, cols] = (yg * lax.rsqrt(ms + NORM_EPS) * normw_ref[:, cols]).astype(out_ref.dtype)


def _ssd(proj, seq_lengths, conv_w, conv_b, dt_bias, a_log, d_skip, ssd_norm_w):
    xbc, dt = proj['xbc'], proj['dt']
    t = xbc.shape[0]
    n_chunks = t // SSD_CHUNK
    halo_per_chunk = SSD_CHUNK // CONV_HALO
    n_halo_blocks = t // CONV_HALO
    flags = _sequence_flags(seq_lengths, SSD_CHUNK)
    pad = LANES - 2 * SSD_HEADS
    dtb = jnp.pad(dt_bias.astype(F32).reshape(1, -1), ((0, 0), (0, pad)))
    alog = jnp.pad(a_log.astype(F32).reshape(1, -1), ((0, 0), (0, pad)))
    scan_scratch = [pltpu.VMEM((SSD_CHUNK, D_INNER), F32), pltpu.VMEM((SSD_GROUPS, D_STATE, GROUP_WIDTH), F32)]

    def const(shape):
        return pl.BlockSpec(shape, lambda i, *_: (0, 0))

    def row(width):
        return pl.BlockSpec((SSD_CHUNK, width), lambda i, *_: (i, 0))

    def rev_row(width):
        return pl.BlockSpec((SSD_CHUNK, width), lambda i, *_: (n_chunks - 1 - i, 0))

    prev = pl.BlockSpec((CONV_HALO, C_XBC), lambda i, *_: (jnp.maximum(i * halo_per_chunk - 1, 0), 0))
    nxt = pl.BlockSpec((CONV_HALO, C_XBC),
                       lambda i, *_: (jnp.minimum((i + 1) * halo_per_chunk, n_halo_blocks - 1), 0))
    y_fwd, xc = pl.pallas_call(
        functools.partial(_ssd_kernel, reverse=False),
        grid_spec=pltpu.PrefetchScalarGridSpec(
            num_scalar_prefetch=1,
            grid=(n_chunks,),
            in_specs=[row(C_XBC), prev, nxt, row(LANES), const((CONV_TAP_ROWS, C_XBC)), const((1, C_XBC)),
                      const((1, LANES)), const((1, LANES))],
            out_specs=[row(D_INNER), row(C_XBC)],
            scratch_shapes=[pltpu.VMEM((SSD_CHUNK + 2 * CONV_HALO, C_XBC), BF16),
                            pltpu.VMEM((SSD_CHUNK, C_XBC), F32)] + scan_scratch,
        ),
        out_shape=[jax.ShapeDtypeStruct((t, D_INNER), BF16), jax.ShapeDtypeStruct((t, C_XBC), BF16)],
        compiler_params=_cparams(("arbitrary",), 40),
        name="ssd_fwd",
    )(flags, xbc, xbc, xbc, dt, jnp.pad(conv_w.astype(F32), ((0, CONV_TAP_ROWS - CONV_K), (0, 0))),
      conv_b.astype(F32).reshape(1, C_XBC), dtb, alog)

    d_lanes = jnp.repeat(d_skip.astype(F32), SSD_HEAD_DIM).reshape(1, D_INNER)
    return pl.pallas_call(
        functools.partial(_ssd_kernel, reverse=True),
        grid_spec=pltpu.PrefetchScalarGridSpec(
            num_scalar_prefetch=1,
            grid=(n_chunks,),
            in_specs=[rev_row(C_XBC), rev_row(LANES), const((1, LANES)), const((1, LANES)), rev_row(D_INNER),
                      rev_row(D_INNER), const((1, D_INNER)), const((1, D_INNER))],
            out_specs=rev_row(D_INNER),
            scratch_shapes=scan_scratch,
        ),
        out_shape=jax.ShapeDtypeStruct((t, D_INNER), BF16),
        compiler_params=_cparams(("arbitrary",), 40),
        name="ssd_bwd",
    )(flags, xc, dt, dtb, alog, y_fwd, proj['silu_z'], d_lanes, ssd_norm_w.astype(F32).reshape(1, D_INNER))


RANK_LANE = TOP_K
ROUTER_TILE = 512
ROUTER_PART = 256


def _out_router_kernel(m_ref, w_ref, xp_ref, xs_ref, nw_ref, rw_ref, rb_ref, x1_ref, h2_ref, ti_ref, tw_ref,
                       cnt_ref, *, n_prompt_tiles):
    i = pl.program_id(0)

    @pl.when(i == 0)
    def _():
        cnt_ref[...] = jnp.zeros_like(cnt_ref)

    def finish(rows):
        acc = jnp.dot(m_ref[rows, :], w_ref[...], preferred_element_type=F32)
        x1 = jnp.where(i < n_prompt_tiles, xp_ref[rows, :], xs_ref[rows, :]) + acc
        x1_ref[rows, :] = x1
        ms = jnp.mean(x1 * x1, axis=-1, keepdims=True)
        h2 = x1 * lax.rsqrt(ms + NORM_EPS) * nw_ref[...]
        h2_ref[rows, :] = h2
        v = jnp.dot(h2.astype(BF16), rw_ref[...], preferred_element_type=F32) + rb_ref[...]
        lane = lax.broadcasted_iota(jnp.int32, v.shape, 1)
        vals, idxs = [], []
        for _ in range(TOP_K):
            top = jnp.max(v, axis=-1, keepdims=True)
            idx = jnp.min(jnp.where(v == top, lane, LANES), axis=-1, keepdims=True)
            vals.append(top)
            idxs.append(idx)
            v = jnp.where(lane == idx, -jnp.inf, v)
        es = [jnp.exp(val - vals[0]) for val in vals]
        total = es[0]
        for e in es[1:]:
            total = total + e
        n_rows = v.shape[0]
        chosen = jnp.zeros(v.shape, F32)
        for kk in range(TOP_K):
            chosen = chosen + (lane == idxs[kk]).astype(F32)
        ri = lax.broadcasted_iota(jnp.int32, (n_rows, n_rows), 0)
        ci = lax.broadcasted_iota(jnp.int32, (n_rows, n_rows), 1)
        before = jnp.dot((ci < ri).astype(BF16), chosen.astype(BF16), preferred_element_type=F32)
        before = before + cnt_ref[0:1, :]
        cnt_ref[...] = cnt_ref[...] + jnp.sum(chosen, axis=0, keepdims=True)
        ti = jnp.zeros(v.shape, jnp.int32)
        tw = jnp.zeros(v.shape, F32)
        for kk in range(TOP_K):
            rank = jnp.sum(jnp.where(lane == idxs[kk], before, 0.0), axis=-1, keepdims=True)
            ti = jnp.where(lane == kk, idxs[kk], ti)
            ti = jnp.where(lane == RANK_LANE + kk, rank.astype(jnp.int32), ti)
            tw = jnp.where(lane == kk, es[kk] / total, tw)
        ti_ref[rows, :] = ti
        tw_ref[rows, :] = tw

    for part in range(m_ref.shape[0] // ROUTER_PART):
        finish(slice(part * ROUTER_PART, (part + 1) * ROUTER_PART))


def _out_router(merged, w_out, xp, xs, norm2_w, router_w, router_b):
    tp, ts = xp.shape[0], xs.shape[0]
    t = tp + ts
    tile = _pick_tile(int(np.gcd(tp, ts)), ROUTER_TILE)
    npt, nst = tp // tile, ts // tile
    p_spec, s_spec = _two_group_specs(tile, npt, nst, D_MODEL)
    rw = jnp.pad(router_w.astype(BF16), ((0, 0), (0, LANES - N_EXPERTS)))
    rb = jnp.pad(router_b.astype(F32).reshape(1, N_EXPERTS), ((0, 0), (0, LANES - N_EXPERTS)),
                 constant_values=-jnp.inf)

    def const(shape):
        return pl.BlockSpec(shape, lambda m: (0, 0), pipeline_mode=pl.Buffered(1))

    def row(width):
        return pl.BlockSpec((tile, width), lambda m: (m, 0))

    return pl.pallas_call(
        functools.partial(_out_router_kernel, n_prompt_tiles=npt),
        grid=(npt + nst,),
        in_specs=[row(D_MODEL), const((D_MODEL, D_MODEL)), p_spec, s_spec, const((1, D_MODEL)),
                  const((D_MODEL, LANES)), const((1, LANES))],
        out_specs=[row(D_MODEL), row(D_MODEL), row(LANES), row(LANES), pl.BlockSpec((8, LANES), lambda m: (0, 0))],
        out_shape=[jax.ShapeDtypeStruct((t, D_MODEL), F32), jax.ShapeDtypeStruct((t, D_MODEL), F32),
                   jax.ShapeDtypeStruct((t, LANES), jnp.int32), jax.ShapeDtypeStruct((t, LANES), F32),
                   jax.ShapeDtypeStruct((8, LANES), F32)],
        compiler_params=_cparams(("arbitrary",), 56),
        name="out_proj_router",
    )(merged, w_out.astype(BF16), xp, xs, norm2_w.astype(F32).reshape(1, D_MODEL), rw, rb)


MOE_ROWS = 512
MOE_FF_TILE = 1024
MOE_GU_CHUNK = 128
MOE_D_CHUNK = 256
MOE_VMEM_MIB = 60


def _moe_kernel(be_ref, nb_ref, tok_hbm, h2_hbm, wgu_hbm, wd_hbm, bgu_ref, bd_ref, out_ref,
                idx_smem, xbuf, xb_scr, wgu_bf, wd_bf, stage_gu, stage_d, idx_sem, row_sem, w_sem):
    b = pl.program_id(0)
    n_used = nb_ref[0]
    expert = be_ref[b]
    expert_changed = jnp.logical_or(b == 0, expert != be_ref[jnp.maximum(b - 1, 0)])

    n_gu = D_MODEL // MOE_GU_CHUNK
    n_d = D_FF // MOE_D_CHUNK

    def weight_copy(k, slot):
        if k < n_gu:
            return pltpu.make_async_copy(wgu_hbm.at[expert, pl.ds(k * MOE_GU_CHUNK, MOE_GU_CHUNK), :],
                                         stage_gu.at[slot], w_sem.at[slot])
        k -= n_gu
        return pltpu.make_async_copy(wd_hbm.at[expert, pl.ds(k * MOE_D_CHUNK, MOE_D_CHUNK), :],
                                     stage_d.at[slot], w_sem.at[2 + slot])

    def load_expert():
        weight_copy(0, 0).start()
        weight_copy(1, 1).start()
        for k in range(n_gu + n_d):
            slot = k % 2
            weight_copy(k, slot).wait()
            if k < n_gu:
                wgu_bf[k * MOE_GU_CHUNK:(k + 1) * MOE_GU_CHUNK, :] = stage_gu[slot].astype(BF16)
            else:
                kd = k - n_gu
                wd_bf[kd * MOE_D_CHUNK:(kd + 1) * MOE_D_CHUNK, :] = stage_d[slot].astype(BF16)
            if k + 2 < n_gu + n_d:
                weight_copy(k + 2, slot).start()

    def idx_copy(block, slot):
        return pltpu.make_async_copy(tok_hbm.at[block], idx_smem.at[slot], idx_sem.at[slot])

    def issue_rows(slot):
        def body(r, carry):
            tok = idx_smem[slot, r]
            pltpu.make_async_copy(h2_hbm.at[pl.ds(tok, 1), :], xbuf.at[slot, pl.ds(r, 1), :],
                                  row_sem.at[slot]).start()
            return carry
        lax.fori_loop(0, MOE_ROWS, body, 0, unroll=True)

    def wait_rows(slot):
        pltpu.make_async_copy(h2_hbm.at[pl.ds(0, MOE_ROWS), :], xbuf.at[slot], row_sem.at[slot]).wait()

    @pl.when(b < n_used)
    def _():
        slot = lax.rem(b, 2)
        nslot = 1 - slot

        @pl.when(b == 0)
        def _():
            idx_copy(0, 0).start()
            idx_copy(0, 0).wait()
            issue_rows(0)

            @pl.when(n_used > 1)
            def _():
                idx_copy(1, 1).start()

        @pl.when(b + 1 < n_used)
        def _():
            idx_copy(b + 1, nslot).wait()
            for s in range(2):
                @pl.when(nslot == s)
                def _():
                    issue_rows(s)

            @pl.when(b + 2 < n_used)
            def _():
                idx_copy(b + 2, slot).start()

        @pl.when(expert_changed)
        def _():
            load_expert()

        wait_rows(slot)
        xb_scr[...] = xbuf[slot].astype(BF16)

        for j in range(D_FF // MOE_FF_TILE):
            cols = slice(j * MOE_FF_TILE, (j + 1) * MOE_FF_TILE)
            up_cols = slice(D_FF + j * MOE_FF_TILE, D_FF + (j + 1) * MOE_FF_TILE)
            xb = xb_scr[...]
            gate = jnp.dot(xb, wgu_bf[:, cols], preferred_element_type=F32) + bgu_ref[0, :, cols]
            up = jnp.dot(xb, wgu_bf[:, up_cols], preferred_element_type=F32) + bgu_ref[0, :, up_cols]
            gate = jnp.minimum(gate, SWIGLU_LIMIT)
            up = jnp.clip(up, -SWIGLU_LIMIT, SWIGLU_LIMIT)
            act = (gate * _sigmoid(SWIGLU_ALPHA * gate) * (up + 1.0)).astype(BF16)
            part = jnp.dot(act, wd_bf[cols, :], preferred_element_type=F32)
            out_ref[...] = (bd_ref[0] if j == 0 else out_ref[...]) + part

    @pl.when(b >= n_used)
    def _():
        out_ref[...] = jnp.zeros_like(out_ref)


def _moe(h2, tok_sorted, block_expert, n_used, w_gate_up, b_gate_up, w_down, b_down):
    n_blocks = tok_sorted.shape[0]
    bgu = b_gate_up.astype(F32).reshape(N_EXPERTS, 1, 2 * D_FF)
    bd = b_down.astype(F32).reshape(N_EXPERTS, 1, D_MODEL)
    grid_spec = pltpu.PrefetchScalarGridSpec(
        num_scalar_prefetch=2,
        grid=(n_blocks,),
        in_specs=[
            pl.BlockSpec(memory_space=pl.ANY),
            pl.BlockSpec(memory_space=pl.ANY),
            pl.BlockSpec(memory_space=pl.ANY),
            pl.BlockSpec(memory_space=pl.ANY),
            pl.BlockSpec((1, 1, 2 * D_FF), lambda b, be, nb: (be[b], 0, 0)),
            pl.BlockSpec((1, 1, D_MODEL), lambda b, be, nb: (be[b], 0, 0)),
        ],
        out_specs=pl.BlockSpec((MOE_ROWS, D_MODEL), lambda b, be, nb: (b, 0)),
        scratch_shapes=[
            pltpu.SMEM((2, MOE_ROWS), jnp.int32),
            pltpu.VMEM((2, MOE_ROWS, D_MODEL), F32),
            pltpu.VMEM((MOE_ROWS, D_MODEL), BF16),
            pltpu.VMEM((D_MODEL, 2 * D_FF), BF16),
            pltpu.VMEM((D_FF, D_MODEL), BF16),
            pltpu.VMEM((2, MOE_GU_CHUNK, 2 * D_FF), F32),
            pltpu.VMEM((2, MOE_D_CHUNK, D_MODEL), F32),
            pltpu.SemaphoreType.DMA((2,)),
            pltpu.SemaphoreType.DMA((2,)),
            pltpu.SemaphoreType.DMA((4,)),
        ],
    )
    return pl.pallas_call(
        _moe_kernel,
        grid_spec=grid_spec,
        out_shape=jax.ShapeDtypeStruct((n_blocks * MOE_ROWS, D_MODEL), F32),
        compiler_params=_cparams(("arbitrary",), MOE_VMEM_MIB),
        name="moe_experts",
    )(block_expert, n_used, tok_sorted, h2, w_gate_up.astype(F32), w_down.astype(F32), bgu, bd)


def _route(top_idx, rank, counts, n_blocks):
    t = top_idx.shape[0]
    flat_e = top_idx.reshape(-1)
    rank = rank.reshape(-1)
    blocks_e = (counts + MOE_ROWS - 1) // MOE_ROWS
    blocks_end = jnp.cumsum(blocks_e)
    blocks_start = blocks_end - blocks_e
    dest = (blocks_start[flat_e] * MOE_ROWS + rank).astype(jnp.int32)
    n_used = blocks_end[-1].astype(jnp.int32)
    flat_tok = jnp.arange(t * TOP_K, dtype=jnp.int32) // TOP_K
    tok_sorted = jnp.zeros((n_blocks * MOE_ROWS,), jnp.int32).at[dest].set(flat_tok, unique_indices=True)
    bidx = jnp.arange(n_blocks, dtype=jnp.int32)
    be = jnp.minimum(jnp.sum((blocks_end[None, :] <= bidx[:, None]).astype(jnp.int32), axis=1), N_EXPERTS - 1)
    be = jnp.where(bidx < n_used, be, be[jnp.maximum(n_used - 1, 0)])
    return tok_sorted.reshape(n_blocks, MOE_ROWS), be, n_used.reshape(1), dest


COMBINE_ROWS = 128


def _combine_kernel(pos_hbm, rows_hbm, x1_ref, tw_ref, yp_ref, ys_ref, pos_smem, gbuf, pos_sem, row_sem,
                    *, n_prompt_tiles):
    i = pl.program_id(0)
    n = pl.num_programs(0)
    slot = lax.rem(i, 2)
    nslot = 1 - slot

    def pos_copy(tile, s):
        return pltpu.make_async_copy(pos_hbm.at[tile], pos_smem.at[s], pos_sem.at[s])

    def issue_rows(s):
        def body(r, carry):
            for kk in range(TOP_K):
                src = pos_smem[s, r * TOP_K + kk]
                pltpu.make_async_copy(rows_hbm.at[pl.ds(src, 1), :], gbuf.at[s, kk, pl.ds(r, 1), :],
                                      row_sem.at[s]).start()
            return carry
        lax.fori_loop(0, COMBINE_ROWS, body, 0, unroll=True)

    def wait_rows(s):
        for kk in range(TOP_K):
            pltpu.make_async_copy(rows_hbm.at[pl.ds(0, COMBINE_ROWS), :], gbuf.at[s, kk], row_sem.at[s]).wait()

    @pl.when(i == 0)
    def _():
        pos_copy(0, 0).start()
        pos_copy(0, 0).wait()
        issue_rows(0)

        @pl.when(n > 1)
        def _():
            pos_copy(1, 1).start()

    @pl.when(i + 1 < n)
    def _():
        pos_copy(i + 1, nslot).wait()
        for s in range(2):
            @pl.when(nslot == s)
            def _():
                issue_rows(s)

        @pl.when(i + 2 < n)
        def _():
            pos_copy(i + 2, slot).start()

    wait_rows(slot)
    tw = tw_ref[...]
    y = x1_ref[...]
    for kk in range(TOP_K):
        y = y + tw[:, kk:kk + 1] * gbuf[slot, kk]

    @pl.when(i < n_prompt_tiles)
    def _():
        yp_ref[...] = y

    @pl.when(i >= n_prompt_tiles)
    def _():
        ys_ref[...] = y


def _combine(x1, expert_rows, dest, top_w, tp, ts):
    t = tp + ts
    tile = _pick_tile(int(np.gcd(tp, ts)), COMBINE_ROWS)
    assert tile == COMBINE_ROWS
    npt, nst = tp // tile, ts // tile
    p_spec, s_spec = _two_group_specs(tile, npt, nst, D_MODEL)
    return pl.pallas_call(
        functools.partial(_combine_kernel, n_prompt_tiles=npt),
        grid=(npt + nst,),
        in_specs=[pl.BlockSpec(memory_space=pl.ANY), pl.BlockSpec(memory_space=pl.ANY),
                  pl.BlockSpec((tile, D_MODEL), lambda m: (m, 0)), pl.BlockSpec((tile, LANES), lambda m: (m, 0))],
        out_specs=[p_spec, s_spec],
        out_shape=[jax.ShapeDtypeStruct((tp, D_MODEL), F32), jax.ShapeDtypeStruct((ts, D_MODEL), F32)],
        scratch_shapes=[
            pltpu.SMEM((2, COMBINE_ROWS * TOP_K), jnp.int32),
            pltpu.VMEM((2, TOP_K, COMBINE_ROWS, D_MODEL), F32),
            pltpu.SemaphoreType.DMA((2,)),
            pltpu.SemaphoreType.DMA((2,)),
        ],
        compiler_params=_cparams(("arbitrary",), 32),
        name="moe_combine",
    )(dest.reshape(t // tile, tile * TOP_K), expert_rows, x1, top_w)


def _sequence_flags(seq_lengths, unit):
    first, last = [], []
    for length in seq_lengths:
        n = length // unit
        first += [1] + [0] * (n - 1)
        last += [0] * (n - 1) + [1]
    return jnp.asarray(np.array([first, last], np.int32))


def _in_projections(h, w_in, q_norm_w, k_norm_w):
    w = w_in.astype(BF16)
    offs = np.cumsum([0, N_HEADS * HEAD_DIM, N_KV_HEADS * HEAD_DIM, N_KV_HEADS * HEAD_DIM, D_INNER, C_XBC,
                      2 * SSD_HEADS, D_MODEL, D_MODEL])
    seg = [w[:, offs[i]:offs[i + 1]] for i in range(8)]
    w_dt = jnp.pad(seg[5], ((0, 0), (0, LANES - 2 * SSD_HEADS)))

    def head_w_spec(tm, tn):
        return [pl.BlockSpec((1, HEAD_DIM), lambda m, j: (0, 0))]

    qw = (q_norm_w.astype(F32) * (LOG2_E / np.sqrt(HEAD_DIM))).reshape(1, HEAD_DIM)
    kw = k_norm_w.astype(F32).reshape(1, HEAD_DIM)
    out = {}
    tm, tn = LINEAR_TILE_M, LINEAR_TILE_N
    out['q'] = _linear([h], [seg[0]], [qw], head_w_spec, _ep_head_norm, BF16, tm, tn, "proj_q")
    out['k'] = _linear([h], [seg[1]], [kw], head_w_spec, _ep_head_norm, BF16, tm, tn, "proj_k")
    out['v'] = _linear([h], [seg[2]], [], _no_aux, _ep_cast, BF16, tm, tn, "proj_v")
    out['silu_z'] = _linear([h], [seg[3]], [], _no_aux, _ep_silu, BF16, tm, tn, "proj_z")
    out['xbc'] = _linear([h], [seg[4]], [], _no_aux, _ep_cast, BF16, tm, tn, "proj_xbc")
    out['dt'] = _linear([h], [w_dt], [], _no_aux, _ep_cast, F32, tm, tn, "proj_dt")
    out['gate_attn'] = _linear([h], [seg[6]], [], _no_aux, _ep_sigmoid, BF16, tm, tn, "proj_gate_attn")
    out['gate_ssd'] = _linear([h], [seg[7]], [], _no_aux, _ep_sigmoid, BF16, tm, tn, "proj_gate_ssd")
    return out


def _pre_attention(x_prompt, x_sample, p):
    xp = x_prompt.reshape(-1, D_MODEL)
    xs = x_sample.reshape(-1, D_MODEL)
    seq_lengths = [x_prompt.shape[1]] * x_prompt.shape[0] + [x_sample.shape[1]] * x_sample.shape[0]
    h = _norm1(xp, xs, p['norm1_w'][0])
    proj = _in_projections(h, p['w_in'][0], p['q_norm_w'][0], p['k_norm_w'][0])
    tq = _pick_tile(int(np.gcd.reduce(seq_lengths)), 512)
    flags = _sequence_flags(seq_lengths, tq)
    proj['attn'] = _attention(proj['q'], proj['k'], proj['v'], p['attn_sink'][0], flags, tq)
    proj['seq_lengths'] = seq_lengths
    if 'conv_w' in p:
        proj['ssd'] = _ssd(proj, seq_lengths, p['conv_w'][0], p['conv_b'][0], p['dt_bias'][0], p['a_log'][0],
                           p['d_skip'][0], p['ssd_norm_w'][0])
    return proj


def kernel(x_prompt, x_sample, norm1_w, w_in, q_norm_w, k_norm_w, attn_sink, conv_w, conv_b, dt_bias, a_log, d_skip, ssd_norm_w, w_attn_proj, w_ssd_proj, w_out, norm2_w, router_w, router_b, w_gate_up, b_gate_up, w_down, b_down):
    assert norm1_w.shape[0] == 1, "single-layer block"
    p = dict(norm1_w=norm1_w, w_in=w_in, q_norm_w=q_norm_w, k_norm_w=k_norm_w, attn_sink=attn_sink,
             conv_w=conv_w, conv_b=conv_b, dt_bias=dt_bias, a_log=a_log, d_skip=d_skip, ssd_norm_w=ssd_norm_w)
    xp = x_prompt.reshape(-1, D_MODEL)
    xs = x_sample.reshape(-1, D_MODEL)
    tp, ts = xp.shape[0], xs.shape[0]
    t = tp + ts
    pre = _pre_attention(x_prompt, x_sample, p)

    def gate_specs(tm, tn):
        return [pl.BlockSpec((tm, tn), lambda m, j: (m, j))] * 2

    merged = _linear([pre['attn'], pre['ssd']], [w_attn_proj[0].astype(BF16), w_ssd_proj[0].astype(BF16)],
                     [pre['gate_attn'], pre['gate_ssd']], gate_specs, _ep_gated_sum, BF16, LINEAR_TILE_M,
                     LINEAR_TILE_N, "branch_merge")
    x1, h2, top_idx, top_w, counts = _out_router(merged, w_out[0], xp, xs, norm2_w[0], router_w[0], router_b[0])

    n_blocks = -(-(t * TOP_K) // MOE_ROWS) + N_EXPERTS
    tok_sorted, block_expert, n_used, dest = _route(top_idx[:, :TOP_K], top_idx[:, RANK_LANE:RANK_LANE + TOP_K],
                                                    counts[0, :N_EXPERTS].astype(jnp.int32), n_blocks)
    expert_rows = _moe(h2, tok_sorted, block_expert, n_used, w_gate_up[0], b_gate_up[0], w_down[0], b_down[0])
    yp, ys = _combine(x1, expert_rows, dest, top_w, tp, ts)
    return yp.reshape(x_prompt.shape), ys.reshape(x_sample.shape)
```

```python
import functools

import numpy as np
import jax
import jax.numpy as jnp
from jax import lax
from jax.experimental import pallas as pl
from jax.experimental.pallas import tpu as pltpu

F32 = jnp.float32
BF16 = jnp.bfloat16
HIGHEST = lax.Precision.HIGHEST

D_MODEL = 2048
N_HEADS = 16
N_KV_HEADS = 4
Q_PER_KV = N_HEADS // N_KV_HEADS
HEAD_DIM = 128
WINDOW = 128
ATTN_BLOCK = 128
D_INNER = 2048
SSD_HEAD_DIM = 64
SSD_HEADS = D_INNER // SSD_HEAD_DIM
SSD_GROUPS = 4
HEADS_PER_GROUP = SSD_HEADS // SSD_GROUPS
GROUP_WIDTH = D_INNER // SSD_GROUPS
D_STATE = 128
CONV_K = 5
SSD_CHUNK = 128
C_XBC = D_INNER + 2 * SSD_GROUPS * D_STATE
N_EXPERTS = 32
TOP_K = 4
D_FF = D_MODEL
SWIGLU_LIMIT = 7.0
SWIGLU_ALPHA = 1.702
NORM_EPS = 1e-6
MASK_VALUE = -1e30
LOG2_E = float(np.log2(np.e))

LANES = 128
BF16_SUBLANES = 16
MIB = 1 << 20


def _cparams(semantics, vmem_mib):
    return pltpu.CompilerParams(dimension_semantics=semantics, vmem_limit_bytes=vmem_mib * MIB)


def _sigmoid(x):
    return 1.0 / (1.0 + jnp.exp(-x))


def _pick_tile(total, preferred):
    t = min(total, preferred)
    while total % t:
        t //= 2
    return t


def _norm1_kernel(xp_ref, xs_ref, w_ref, o_ref, *, n_prompt_tiles):
    m = pl.program_id(0)

    def body(x_ref):
        x = x_ref[...]
        ms = jnp.mean(x * x, axis=-1, keepdims=True)
        o_ref[...] = (x * lax.rsqrt(ms + NORM_EPS) * w_ref[...]).astype(BF16)

    @pl.when(m < n_prompt_tiles)
    def _():
        body(xp_ref)

    @pl.when(m >= n_prompt_tiles)
    def _():
        body(xs_ref)


def _two_group_specs(tile, n_prompt_tiles, n_sample_tiles, width):
    p_spec = pl.BlockSpec((tile, width), lambda m: (jnp.minimum(m, n_prompt_tiles - 1), 0))
    s_spec = pl.BlockSpec((tile, width), lambda m: (jnp.maximum(m - n_prompt_tiles, 0), 0))
    return p_spec, s_spec


def _norm1(xp, xs, w):
    tp, ts = xp.shape[0], xs.shape[0]
    tile = _pick_tile(int(np.gcd(tp, ts)), 512)
    npt, nst = tp // tile, ts // tile
    p_spec, s_spec = _two_group_specs(tile, npt, nst, D_MODEL)
    return pl.pallas_call(
        functools.partial(_norm1_kernel, n_prompt_tiles=npt),
        grid=(npt + nst,),
        in_specs=[p_spec, s_spec, pl.BlockSpec((1, D_MODEL), lambda m: (0, 0))],
        out_specs=pl.BlockSpec((tile, D_MODEL), lambda m: (m, 0)),
        out_shape=jax.ShapeDtypeStruct((tp + ts, D_MODEL), BF16),
        compiler_params=_cparams(("parallel",), 40),
        name="norm1",
    )(xp, xs, w.reshape(1, D_MODEL))


LINEAR_TILE_M = 1024
LINEAR_TILE_N = 1024


def _linear_kernel(*refs, n_lhs, n_aux, epilogue):
    lhs = refs[:n_lhs]
    rhs = refs[n_lhs:2 * n_lhs]
    aux = refs[2 * n_lhs:2 * n_lhs + n_aux]
    out = refs[2 * n_lhs + n_aux]
    accs = [jnp.dot(l[...], r[...], preferred_element_type=F32) for l, r in zip(lhs, rhs)]
    epilogue(accs, aux, out)


def _ep_cast(accs, aux, out):
    out[...] = accs[0].astype(out.dtype)


def _ep_silu(accs, aux, out):
    a = accs[0]
    out[...] = (a * _sigmoid(a)).astype(out.dtype)


def _ep_sigmoid(accs, aux, out):
    out[...] = _sigmoid(accs[0]).astype(out.dtype)


def _ep_head_norm(accs, aux, out):
    a = accs[0]
    w = aux[0][...]
    for j in range(a.shape[1] // HEAD_DIM):
        s = a[:, j * HEAD_DIM:(j + 1) * HEAD_DIM]
        ms = jnp.mean(s * s, axis=-1, keepdims=True)
        out[:, j * HEAD_DIM:(j + 1) * HEAD_DIM] = (s * lax.rsqrt(ms + NORM_EPS) * w).astype(out.dtype)


def _ep_gated_sum(accs, aux, out):
    out[...] = (aux[0][...].astype(F32) * accs[0] + aux[1][...].astype(F32) * accs[1]).astype(out.dtype)


def _linear(lhs_list, rhs_list, aux_list, aux_specs, epilogue, out_dtype, tm, tn, name):
    t, k = lhs_list[0].shape
    n = rhs_list[0].shape[1]
    tm = _pick_tile(t, tm)
    tn = _pick_tile(n, tn)
    in_specs = ([pl.BlockSpec((tm, k), lambda m, j: (m, 0)) for _ in lhs_list]
                + [pl.BlockSpec((k, tn), lambda m, j: (0, j)) for _ in rhs_list]
                + list(aux_specs(tm, tn)))
    return pl.pallas_call(
        functools.partial(_linear_kernel, n_lhs=len(lhs_list), n_aux=len(aux_list), epilogue=epilogue),
        grid=(t // tm, n // tn),
        in_specs=in_specs,
        out_specs=pl.BlockSpec((tm, tn), lambda m, j: (m, j)),
        out_shape=jax.ShapeDtypeStruct((t, n), out_dtype),
        compiler_params=_cparams(("parallel", "arbitrary"), 56),
        name=name,
    )(*lhs_list, *rhs_list, *aux_list)


def _no_aux(tm, tn):
    return []


ATTN_CHUNK = 1024


def _attn_kernel(flags_ref, slope_ref, sink_ref, q_ref, k_ref, kp_ref, kn_ref, v_ref, vp_ref, vn_ref, o_ref,
                 *, n_sub):
    i = pl.program_id(0)
    g = pl.program_id(1)
    has_prev = flags_ref[0, i] == 0
    has_next = flags_ref[1, i] == 0

    qi = lax.broadcasted_iota(jnp.int32, (ATTN_BLOCK, 3 * ATTN_BLOCK), 0)
    kj = lax.broadcasted_iota(jnp.int32, (ATTN_BLOCK, 3 * ATTN_BLOCK), 1)
    dist = jnp.abs(ATTN_BLOCK + qi - kj)
    in_window = dist <= WINDOW
    dist_f = dist.astype(F32)
    is_prev_blk = kj < ATTN_BLOCK
    is_next_blk = kj >= 2 * ATTN_BLOCK

    for j in range(n_sub):
        rows = slice(j * ATTN_BLOCK, (j + 1) * ATTN_BLOCK)
        prev_rows = slice((j - 1) * ATTN_BLOCK, j * ATTN_BLOCK)
        next_rows = slice((j + 1) * ATTN_BLOCK, (j + 2) * ATTN_BLOCK)
        k_prev = kp_ref[...] if j == 0 else k_ref[prev_rows, :]
        v_prev = vp_ref[...] if j == 0 else v_ref[prev_rows, :]
        k_next = kn_ref[...] if j == n_sub - 1 else k_ref[next_rows, :]
        v_next = vn_ref[...] if j == n_sub - 1 else v_ref[next_rows, :]
        k_band = jnp.concatenate([k_prev, k_ref[rows, :], k_next], axis=0)
        v_band = jnp.concatenate([v_prev, v_ref[rows, :], v_next], axis=0)
        valid = in_window
        if j == 0:
            valid = valid & (has_prev | jnp.logical_not(is_prev_blk))
        if j == n_sub - 1:
            valid = valid & (has_next | jnp.logical_not(is_next_blk))
        q_stack = jnp.concatenate(
            [q_ref[rows, r * HEAD_DIM:(r + 1) * HEAD_DIM] for r in range(Q_PER_KV)], axis=0)
        s_all = lax.dot_general(q_stack, k_band, (((1,), (1,)), ((), ())), preferred_element_type=F32)
        for r in range(Q_PER_KV):
            head = g * Q_PER_KV + r
            s = s_all[r * ATTN_BLOCK:(r + 1) * ATTN_BLOCK, :] - slope_ref[head] * dist_f
            s = jnp.where(valid, s, MASK_VALUE)
            sink = sink_ref[head]
            m = jnp.maximum(jnp.max(s, axis=-1, keepdims=True), sink)
            p = jnp.exp2(s - m)
            denom = jnp.sum(p, axis=-1, keepdims=True) + jnp.exp2(sink - m)
            o = jnp.dot(p.astype(BF16), v_band, preferred_element_type=F32)
            o_ref[rows, r * HEAD_DIM:(r + 1) * HEAD_DIM] = (o / denom).astype(o_ref.dtype)


def _attention(q, k, v, sink, chunk_flags, tq):
    t = q.shape[0]
    n_chunks = t // tq
    n_blocks = t // ATTN_BLOCK
    sub = tq // ATTN_BLOCK
    slopes = jnp.asarray(LOG2_E * 2.0 ** (-8.0 * (np.arange(N_HEADS, dtype=np.float32) + 1.0) / N_HEADS), F32)
    gw = Q_PER_KV * HEAD_DIM

    def own(width):
        return pl.BlockSpec((tq, width), lambda i, g, *_: (i, g))

    prev = pl.BlockSpec((ATTN_BLOCK, HEAD_DIM), lambda i, g, *_: (jnp.maximum(i * sub - 1, 0), g))
    nxt = pl.BlockSpec((ATTN_BLOCK, HEAD_DIM), lambda i, g, *_: (jnp.minimum((i + 1) * sub, n_blocks - 1), g))
    grid_spec = pltpu.PrefetchScalarGridSpec(
        num_scalar_prefetch=3,
        grid=(n_chunks, N_KV_HEADS),
        in_specs=[own(gw), own(HEAD_DIM), prev, nxt, own(HEAD_DIM), prev, nxt],
        out_specs=own(gw),
    )
    return pl.pallas_call(
        functools.partial(_attn_kernel, n_sub=sub),
        grid_spec=grid_spec,
        out_shape=jax.ShapeDtypeStruct((t, N_HEADS * HEAD_DIM), BF16),
        compiler_params=_cparams(("parallel", "arbitrary"), 32),
        name="banded_attention",
    )(chunk_flags, slopes, sink.astype(F32) * LOG2_E, q, k, k, k, v, v, v)


CONV_HALO = BF16_SUBLANES
CONV_PAD = CONV_K // 2
CONV_TAP_ROWS = 8
CONV_COL_TILE = 512
HEAD_PAIR_WIDTH = 2 * SSD_HEAD_DIM


def _ssd_kernel(flags_ref, *rest, reverse):
    i = pl.program_id(0)
    c = pl.num_programs(0) - 1 - i if reverse else i
    seq_first = flags_ref[0, c] == 1
    seq_last = flags_ref[1, c] == 1
    L = SSD_CHUNK

    if reverse:
        xc_ref, dt_ref, dtb_ref, alog_ref, yf_ref, sz_ref, dskip_ref, normw_ref, out_ref, y_scr, state_scr = rest

        def xc(cols):
            return xc_ref[:, cols].astype(F32)
    else:
        (xbc_ref, xprev_ref, xnext_ref, dt_ref, convw_ref, convb_ref, dtb_ref, alog_ref, out_ref, xc_out_ref,
         ext_scr, xc_scr, y_scr, state_scr) = rest

        def xc(cols):
            return xc_scr[:, cols]

        zero_halo = jnp.zeros((CONV_HALO, C_XBC), BF16)
        ext_scr[0:CONV_HALO, :] = jnp.where(seq_first, zero_halo, xprev_ref[...])
        ext_scr[CONV_HALO:CONV_HALO + L, :] = xbc_ref[...]
        ext_scr[CONV_HALO + L:, :] = jnp.where(seq_last, zero_halo, xnext_ref[...])
        out_row = lax.broadcasted_iota(jnp.int32, (L, L + 2 * CONV_HALO), 0)
        src_row = lax.broadcasted_iota(jnp.int32, (L, L + 2 * CONV_HALO), 1)
        for ct in range(C_XBC // CONV_COL_TILE):
            cols = slice(ct * CONV_COL_TILE, (ct + 1) * CONV_COL_TILE)
            ext = ext_scr[:, cols]
            acc = convb_ref[:, cols] + xbc_ref[:, cols].astype(F32) * convw_ref[CONV_PAD:CONV_PAD + 1, cols]
            for j in range(CONV_K):
                if j == CONV_PAD:
                    continue
                shift = (src_row == out_row + (CONV_HALO - CONV_PAD + j)).astype(BF16)
                acc = acc + jnp.dot(shift, ext, preferred_element_type=F32) * convw_ref[j:j + 1, cols]
            conv = acc * _sigmoid(acc)
            xc_scr[:, cols] = conv
            xc_out_ref[:, cols] = conv.astype(xc_out_ref.dtype)

    col = lax.broadcasted_iota(jnp.int32, (1, LANES), 1)
    a_neg = jnp.where(col < 2 * SSD_HEADS, -jnp.exp(alog_ref[...]), 0.0)
    xdt = dt_ref[...] + dtb_ref[...]
    dt = jnp.maximum(xdt, 0.0) + jnp.log1p(jnp.exp(-jnp.abs(xdt)))
    a = dt * (a_neg * LOG2_E)
    ri = lax.broadcasted_iota(jnp.int32, (L, L), 0)
    ci = lax.broadcasted_iota(jnp.int32, (L, L), 1)
    causal = (ri <= ci) if reverse else (ri >= ci)
    a_cum = jnp.dot(causal.astype(F32), a, precision=HIGHEST, preferred_element_type=F32)
    a_cum_t = a_cum.T
    dt_t = dt.T
    edge = 0 if reverse else L - 1
    a_total = a_cum[edge:edge + 1, :]
    w_state = dt * jnp.exp2(a_total - a_cum)
    dir_off = SSD_HEADS if reverse else 0
    hr = lax.broadcasted_iota(jnp.int32, (LANES, D_INNER), 0)
    hc = lax.broadcasted_iota(jnp.int32, (LANES, D_INNER), 1)
    expand = (hr == dir_off + lax.shift_right_logical(hc, int(np.log2(SSD_HEAD_DIM)))).astype(F32)
    chunk_decay = jnp.dot(jnp.broadcast_to(jnp.exp2(a_total), (8, LANES)), expand, precision=HIGHEST,
                          preferred_element_type=F32)[0:1, :]

    @pl.when(seq_last if reverse else seq_first)
    def _():
        state_scr[...] = jnp.zeros_like(state_scr)

    lo = lax.broadcasted_iota(jnp.int32, (L, LANES), 1) < SSD_HEAD_DIM
    for g in range(SSD_GROUPS):
        b_g = xc(slice(D_INNER + g * D_STATE, D_INNER + (g + 1) * D_STATE))
        c_g = xc(slice(D_INNER + (SSD_GROUPS + g) * D_STATE, D_INNER + (SSD_GROUPS + g + 1) * D_STATE))
        c_bf = c_g.astype(BF16)
        cb = lax.dot_general(c_bf, b_g.astype(BF16), (((1,), (1,)), ((), ())), preferred_element_type=F32)
        b_t = b_g.T.astype(BF16)
        state = state_scr[g]
        y_off = jnp.dot(c_bf, state.astype(BF16), preferred_element_type=F32)
        xw_parts = []
        for p in range(HEADS_PER_GROUP // 2):
            gcols = slice(g * GROUP_WIDTH + p * HEAD_PAIR_WIDTH, g * GROUP_WIDTH + (p + 1) * HEAD_PAIR_WIDTH)
            m_parts, e_parts, w_parts = [], [], []
            for hh in range(2):
                k = dir_off + g * HEADS_PER_GROUP + 2 * p + hh
                colb = jnp.broadcast_to(a_cum[:, k:k + 1], (L, L))
                rowb = jnp.broadcast_to(a_cum_t[k:k + 1, :], (L, L))
                dtrow = jnp.broadcast_to(dt_t[k:k + 1, :], (L, L))
                decay = jnp.exp2(jnp.where(causal, colb - rowb, -jnp.inf))
                m_parts.append((cb * decay * dtrow).astype(BF16))
                e_parts.append(jnp.exp2(colb))
                w_parts.append(jnp.broadcast_to(w_state[:, k:k + 1], (L, LANES)))
            x_pair = xc(gcols)
            rhs = jnp.concatenate([jnp.where(lo, x_pair, 0.0), jnp.where(lo, 0.0, x_pair)], axis=0).astype(BF16)
            y = jnp.dot(jnp.concatenate(m_parts, axis=1), rhs, preferred_element_type=F32)
            y = y + y_off[:, p * HEAD_PAIR_WIDTH:(p + 1) * HEAD_PAIR_WIDTH] * jnp.where(lo, e_parts[0], e_parts[1])
            y_scr[:, gcols] = y
            xw_parts.append((x_pair * jnp.where(lo, w_parts[0], w_parts[1])).astype(BF16))
        xw = jnp.concatenate(xw_parts, axis=1)
        state_scr[g] = (state * chunk_decay[:, g * GROUP_WIDTH:(g + 1) * GROUP_WIDTH]
                        + jnp.dot(b_t, xw, preferred_element_type=F32))

    if not reverse:
        out_ref[...] = y_scr[...].astype(out_ref.dtype)
    else:
        for g in range(SSD_GROUPS):
            cols = slice(g * GROUP_WIDTH, (g + 1) * GROUP_WIDTH)
            y = y_scr[:, cols] + yf_ref[:, cols].astype(F32) + xc(cols) * dskip_ref[:, cols]
            yg = y * sz_ref[:, cols].astype(F32)
            ms = jnp.mean(yg * yg, axis=-1, keepdims=True)
            out_ref[:, cols] = (yg * lax.rsqrt(ms + NORM_EPS) * normw_ref[:, cols]).astype(out_ref.dtype)


def _ssd(proj, seq_lengths, conv_w, conv_b, dt_bias, a_log, d_skip, ssd_norm_w):
    xbc, dt = proj['xbc'], proj['dt']
    t = xbc.shape[0]
    n_chunks = t // SSD_CHUNK
    halo_per_chunk = SSD_CHUNK // CONV_HALO
    n_halo_blocks = t // CONV_HALO
    flags = _sequence_flags(seq_lengths, SSD_CHUNK)
    pad = LANES - 2 * SSD_HEADS
    dtb = jnp.pad(dt_bias.astype(F32).reshape(1, -1), ((0, 0), (0, pad)))
    alog = jnp.pad(a_log.astype(F32).reshape(1, -1), ((0, 0), (0, pad)))
    scan_scratch = [pltpu.VMEM((SSD_CHUNK, D_INNER), F32), pltpu.VMEM((SSD_GROUPS, D_STATE, GROUP_WIDTH), F32)]

    def const(shape):
        return pl.BlockSpec(shape, lambda i, *_: (0, 0))

    def row(width):
        return pl.BlockSpec((SSD_CHUNK, width), lambda i, *_: (i, 0))

    def rev_row(width):
        return pl.BlockSpec((SSD_CHUNK, width), lambda i, *_: (n_chunks - 1 - i, 0))

    prev = pl.BlockSpec((CONV_HALO, C_XBC), lambda i, *_: (jnp.maximum(i * halo_per_chunk - 1, 0), 0))
    nxt = pl.BlockSpec((CONV_HALO, C_XBC),
                       lambda i, *_: (jnp.minimum((i + 1) * halo_per_chunk, n_halo_blocks - 1), 0))
    y_fwd, xc = pl.pallas_call(
        functools.partial(_ssd_kernel, reverse=False),
        grid_spec=pltpu.PrefetchScalarGridSpec(
            num_scalar_prefetch=1,
            grid=(n_chunks,),
            in_specs=[row(C_XBC), prev, nxt, row(LANES), const((CONV_TAP_ROWS, C_XBC)), const((1, C_XBC)),
                      const((1, LANES)), const((1, LANES))],
            out_specs=[row(D_INNER), row(C_XBC)],
            scratch_shapes=[pltpu.VMEM((SSD_CHUNK + 2 * CONV_HALO, C_XBC), BF16),
                            pltpu.VMEM((SSD_CHUNK, C_XBC), F32)] + scan_scratch,
        ),
        out_shape=[jax.ShapeDtypeStruct((t, D_INNER), BF16), jax.ShapeDtypeStruct((t, C_XBC), BF16)],
        compiler_params=_cparams(("arbitrary",), 40),
        name="ssd_fwd",
    )(flags, xbc, xbc, xbc, dt, jnp.pad(conv_w.astype(F32), ((0, CONV_TAP_ROWS - CONV_K), (0, 0))),
      conv_b.astype(F32).reshape(1, C_XBC), dtb, alog)

    d_lanes = jnp.repeat(d_skip.astype(F32), SSD_HEAD_DIM).reshape(1, D_INNER)
    return pl.pallas_call(
        functools.partial(_ssd_kernel, reverse=True),
        grid_spec=pltpu.PrefetchScalarGridSpec(
            num_scalar_prefetch=1,
            grid=(n_chunks,),
            in_specs=[rev_row(C_XBC), rev_row(LANES), const((1, LANES)), const((1, LANES)), rev_row(D_INNER),
                      rev_row(D_INNER), const((1, D_INNER)), const((1, D_INNER))],
            out_specs=rev_row(D_INNER),
            scratch_shapes=scan_scratch,
        ),
        out_shape=jax.ShapeDtypeStruct((t, D_INNER), BF16),
        compiler_params=_cparams(("arbitrary",), 40),
        name="ssd_bwd",
    )(flags, xc, dt, dtb, alog, y_fwd, proj['silu_z'], d_lanes, ssd_norm_w.astype(F32).reshape(1, D_INNER))


RANK_LANE = TOP_K
ROUTER_TILE = 512
ROUTER_PART = 256


def _out_router_kernel(m_ref, w_ref, xp_ref, xs_ref, nw_ref, rw_ref, rb_ref, x1_ref, h2_ref, ti_ref, tw_ref,
                       cnt_ref, *, n_prompt_tiles):
    i = pl.program_id(0)

    @pl.when(i == 0)
    def _():
        cnt_ref[...] = jnp.zeros_like(cnt_ref)

    def finish(rows):
        acc = jnp.dot(m_ref[rows, :], w_ref[...], preferred_element_type=F32)
        x1 = jnp.where(i < n_prompt_tiles, xp_ref[rows, :], xs_ref[rows, :]) + acc
        x1_ref[rows, :] = x1
        ms = jnp.mean(x1 * x1, axis=-1, keepdims=True)
        h2 = x1 * lax.rsqrt(ms + NORM_EPS) * nw_ref[...]
        h2_ref[rows, :] = h2
        v = jnp.dot(h2.astype(BF16), rw_ref[...], preferred_element_type=F32) + rb_ref[...]
        lane = lax.broadcasted_iota(jnp.int32, v.shape, 1)
        vals, idxs = [], []
        for _ in range(TOP_K):
            top = jnp.max(v, axis=-1, keepdims=True)
            idx = jnp.min(jnp.where(v == top, lane, LANES), axis=-1, keepdims=True)
            vals.append(top)
            idxs.append(idx)
            v = jnp.where(lane == idx, -jnp.inf, v)
        es = [jnp.exp(val - vals[0]) for val in vals]
        total = es[0]
        for e in es[1:]:
            total = total + e
        n_rows = v.shape[0]
        chosen = jnp.zeros(v.shape, F32)
        for kk in range(TOP_K):
            chosen = chosen + (lane == idxs[kk]).astype(F32)
        ri = lax.broadcasted_iota(jnp.int32, (n_rows, n_rows), 0)
        ci = lax.broadcasted_iota(jnp.int32, (n_rows, n_rows), 1)
        before = jnp.dot((ci < ri).astype(BF16), chosen.astype(BF16), preferred_element_type=F32)
        before = before + cnt_ref[0:1, :]
        cnt_ref[...] = cnt_ref[...] + jnp.sum(chosen, axis=0, keepdims=True)
        ti = jnp.zeros(v.shape, jnp.int32)
        tw = jnp.zeros(v.shape, F32)
        for kk in range(TOP_K):
            rank = jnp.sum(jnp.where(lane == idxs[kk], before, 0.0), axis=-1, keepdims=True)
            ti = jnp.where(lane == kk, idxs[kk], ti)
            ti = jnp.where(lane == RANK_LANE + kk, rank.astype(jnp.int32), ti)
            tw = jnp.where(lane == kk, es[kk] / total, tw)
        ti_ref[rows, :] = ti
        tw_ref[rows, :] = tw

    for part in range(m_ref.shape[0] // ROUTER_PART):
        finish(slice(part * ROUTER_PART, (part + 1) * ROUTER_PART))


def _out_router(merged, w_out, xp, xs, norm2_w, router_w, router_b):
    tp, ts = xp.shape[0], xs.shape[0]
    t = tp + ts
    tile = _pick_tile(int(np.gcd(tp, ts)), ROUTER_TILE)
    npt, nst = tp // tile, ts // tile
    p_spec, s_spec = _two_group_specs(tile, npt, nst, D_MODEL)
    rw = jnp.pad(router_w.astype(BF16), ((0, 0), (0, LANES - N_EXPERTS)))
    rb = jnp.pad(router_b.astype(F32).reshape(1, N_EXPERTS), ((0, 0), (0, LANES - N_EXPERTS)),
                 constant_values=-jnp.inf)

    def const(shape):
        return pl.BlockSpec(shape, lambda m: (0, 0), pipeline_mode=pl.Buffered(1))

    def row(width):
        return pl.BlockSpec((tile, width), lambda m: (m, 0))

    return pl.pallas_call(
        functools.partial(_out_router_kernel, n_prompt_tiles=npt),
        grid=(npt + nst,),
        in_specs=[row(D_MODEL), const((D_MODEL, D_MODEL)), p_spec, s_spec, const((1, D_MODEL)),
                  const((D_MODEL, LANES)), const((1, LANES))],
        out_specs=[row(D_MODEL), row(D_MODEL), row(LANES), row(LANES), pl.BlockSpec((8, LANES), lambda m: (0, 0))],
        out_shape=[jax.ShapeDtypeStruct((t, D_MODEL), F32), jax.ShapeDtypeStruct((t, D_MODEL), F32),
                   jax.ShapeDtypeStruct((t, LANES), jnp.int32), jax.ShapeDtypeStruct((t, LANES), F32),
                   jax.ShapeDtypeStruct((8, LANES), F32)],
        compiler_params=_cparams(("arbitrary",), 56),
        name="out_proj_router",
    )(merged, w_out.astype(BF16), xp, xs, norm2_w.astype(F32).reshape(1, D_MODEL), rw, rb)


MOE_ROWS = 512
MOE_FF_TILE = 1024
MOE_GU_CHUNK = 128
MOE_D_CHUNK = 256
MOE_VMEM_MIB = 60


def _moe_kernel(be_ref, nb_ref, tok_hbm, h2_hbm, wgu_hbm, wd_hbm, bgu_ref, bd_ref, out_ref,
                idx_smem, xbuf, xb_scr, wgu_bf, wd_bf, stage_gu, stage_d, idx_sem, row_sem, w_sem):
    b = pl.program_id(0)
    n_used = nb_ref[0]
    expert = be_ref[b]
    expert_changed = jnp.logical_or(b == 0, expert != be_ref[jnp.maximum(b - 1, 0)])

    n_gu = D_MODEL // MOE_GU_CHUNK
    n_d = D_FF // MOE_D_CHUNK

    def weight_copy(k, slot):
        if k < n_gu:
            return pltpu.make_async_copy(wgu_hbm.at[expert, pl.ds(k * MOE_GU_CHUNK, MOE_GU_CHUNK), :],
                                         stage_gu.at[slot], w_sem.at[slot])
        k -= n_gu
        return pltpu.make_async_copy(wd_hbm.at[expert, pl.ds(k * MOE_D_CHUNK, MOE_D_CHUNK), :],
                                     stage_d.at[slot], w_sem.at[2 + slot])

    def start_expert_load():
        weight_copy(0, 0).start()
        weight_copy(1, 1).start()

    def finish_expert_load():
        for k in range(n_gu + n_d):
            slot = k % 2
            weight_copy(k, slot).wait()
            if k < n_gu:
                wgu_bf[k * MOE_GU_CHUNK:(k + 1) * MOE_GU_CHUNK, :] = stage_gu[slot].astype(BF16)
            else:
                kd = k - n_gu
                wd_bf[kd * MOE_D_CHUNK:(kd + 1) * MOE_D_CHUNK, :] = stage_d[slot].astype(BF16)
            if k + 2 < n_gu + n_d:
                weight_copy(k + 2, slot).start()

    def idx_copy(block, slot):
        return pltpu.make_async_copy(tok_hbm.at[block], idx_smem.at[slot], idx_sem.at[slot])

    def issue_rows(slot):
        def body(r, carry):
            tok = idx_smem[slot, r]
            pltpu.make_async_copy(h2_hbm.at[pl.ds(tok, 1), :], xbuf.at[slot, pl.ds(r, 1), :],
                                  row_sem.at[slot]).start()
            return carry
        lax.fori_loop(0, MOE_ROWS, body, 0, unroll=True)

    def wait_rows(slot):
        pltpu.make_async_copy(h2_hbm.at[pl.ds(0, MOE_ROWS), :], xbuf.at[slot], row_sem.at[slot]).wait()

    @pl.when(b < n_used)
    def _():
        slot = lax.rem(b, 2)
        nslot = 1 - slot

        @pl.when(expert_changed)
        def _():
            start_expert_load()

        @pl.when(b == 0)
        def _():
            idx_copy(0, 0).start()
            idx_copy(0, 0).wait()
            issue_rows(0)

            @pl.when(n_used > 1)
            def _():
                idx_copy(1, 1).start()

        @pl.when(b + 1 < n_used)
        def _():
            idx_copy(b + 1, nslot).wait()
            for s in range(2):
                @pl.when(nslot == s)
                def _():
                    issue_rows(s)

            @pl.when(b + 2 < n_used)
            def _():
                idx_copy(b + 2, slot).start()

        @pl.when(expert_changed)
        def _():
            finish_expert_load()

        wait_rows(slot)
        xb_scr[...] = xbuf[slot].astype(BF16)

        for j in range(D_FF // MOE_FF_TILE):
            cols = slice(j * MOE_FF_TILE, (j + 1) * MOE_FF_TILE)
            up_cols = slice(D_FF + j * MOE_FF_TILE, D_FF + (j + 1) * MOE_FF_TILE)
            xb = xb_scr[...]
            gate = jnp.dot(xb, wgu_bf[:, cols], preferred_element_type=F32) + bgu_ref[0, :, cols]
            up = jnp.dot(xb, wgu_bf[:, up_cols], preferred_element_type=F32) + bgu_ref[0, :, up_cols]
            gate = jnp.minimum(gate, SWIGLU_LIMIT)
            up = jnp.clip(up, -SWIGLU_LIMIT, SWIGLU_LIMIT)
            act = (gate * _sigmoid(SWIGLU_ALPHA * gate) * (up + 1.0)).astype(BF16)
            part = jnp.dot(act, wd_bf[cols, :], preferred_element_type=F32)
            out_ref[...] = (bd_ref[0] if j == 0 else out_ref[...]) + part

    @pl.when(b >= n_used)
    def _():
        out_ref[...] = jnp.zeros_like(out_ref)


def _moe(h2, tok_sorted, block_expert, n_used, w_gate_up, b_gate_up, w_down, b_down):
    n_blocks = tok_sorted.shape[0]
    bgu = b_gate_up.astype(F32).reshape(N_EXPERTS, 1, 2 * D_FF)
    bd = b_down.astype(F32).reshape(N_EXPERTS, 1, D_MODEL)
    grid_spec = pltpu.PrefetchScalarGridSpec(
        num_scalar_prefetch=2,
        grid=(n_blocks,),
        in_specs=[
            pl.BlockSpec(memory_space=pl.ANY),
            pl.BlockSpec(memory_space=pl.ANY),
            pl.BlockSpec(memory_space=pl.ANY),
            pl.BlockSpec(memory_space=pl.ANY),
            pl.BlockSpec((1, 1, 2 * D_FF), lambda b, be, nb: (be[b], 0, 0)),
            pl.BlockSpec((1, 1, D_MODEL), lambda b, be, nb: (be[b], 0, 0)),
        ],
        out_specs=pl.BlockSpec((MOE_ROWS, D_MODEL), lambda b, be, nb: (b, 0)),
        scratch_shapes=[
            pltpu.SMEM((2, MOE_ROWS), jnp.int32),
            pltpu.VMEM((2, MOE_ROWS, D_MODEL), F32),
            pltpu.VMEM((MOE_ROWS, D_MODEL), BF16),
            pltpu.VMEM((D_MODEL, 2 * D_FF), BF16),
            pltpu.VMEM((D_FF, D_MODEL), BF16),
            pltpu.VMEM((2, MOE_GU_CHUNK, 2 * D_FF), F32),
            pltpu.VMEM((2, MOE_D_CHUNK, D_MODEL), F32),
            pltpu.SemaphoreType.DMA((2,)),
            pltpu.SemaphoreType.DMA((2,)),
            pltpu.SemaphoreType.DMA((4,)),
        ],
    )
    return pl.pallas_call(
        _moe_kernel,
        grid_spec=grid_spec,
        out_shape=jax.ShapeDtypeStruct((n_blocks * MOE_ROWS, D_MODEL), F32),
        compiler_params=_cparams(("arbitrary",), MOE_VMEM_MIB),
        name="moe_experts",
    )(block_expert, n_used, tok_sorted, h2, w_gate_up.astype(F32), w_down.astype(F32), bgu, bd)


def _route(top_idx, rank, counts, n_blocks):
    t = top_idx.shape[0]
    flat_e = top_idx.reshape(-1)
    rank = rank.reshape(-1)
    blocks_e = (counts + MOE_ROWS - 1) // MOE_ROWS
    blocks_end = jnp.cumsum(blocks_e)
    blocks_start = blocks_end - blocks_e
    dest = (blocks_start[flat_e] * MOE_ROWS + rank).astype(jnp.int32)
    n_used = blocks_end[-1].astype(jnp.int32)
    flat_tok = jnp.arange(t * TOP_K, dtype=jnp.int32) // TOP_K
    tok_sorted = jnp.zeros((n_blocks * MOE_ROWS,), jnp.int32).at[dest].set(flat_tok, unique_indices=True)
    bidx = jnp.arange(n_blocks, dtype=jnp.int32)
    be = jnp.minimum(jnp.sum((blocks_end[None, :] <= bidx[:, None]).astype(jnp.int32), axis=1), N_EXPERTS - 1)
    be = jnp.where(bidx < n_used, be, be[jnp.maximum(n_used - 1, 0)])
    return tok_sorted.reshape(n_blocks, MOE_ROWS), be, n_used.reshape(1), dest


COMBINE_ROWS = 128


def _combine_kernel(pos_hbm, rows_hbm, x1_ref, tw_ref, yp_ref, ys_ref, pos_smem, gbuf, pos_sem, row_sem,
                    *, n_prompt_tiles):
    i = pl.program_id(0)
    n = pl.num_programs(0)
    slot = lax.rem(i, 2)
    nslot = 1 - slot

    def pos_copy(tile, s):
        return pltpu.make_async_copy(pos_hbm.at[tile], pos_smem.at[s], pos_sem.at[s])

    def issue_rows(s):
        def body(r, carry):
            for kk in range(TOP_K):
                src = pos_smem[s, r * TOP_K + kk]
                pltpu.make_async_copy(rows_hbm.at[pl.ds(src, 1), :], gbuf.at[s, kk, pl.ds(r, 1), :],
                                      row_sem.at[s]).start()
            return carry
        lax.fori_loop(0, COMBINE_ROWS, body, 0, unroll=True)

    def wait_rows(s):
        for kk in range(TOP_K):
            pltpu.make_async_copy(rows_hbm.at[pl.ds(0, COMBINE_ROWS), :], gbuf.at[s, kk], row_sem.at[s]).wait()

    @pl.when(i == 0)
    def _():
        pos_copy(0, 0).start()
        pos_copy(0, 0).wait()
        issue_rows(0)

        @pl.when(n > 1)
        def _():
            pos_copy(1, 1).start()

    @pl.when(i + 1 < n)
    def _():
        pos_copy(i + 1, nslot).wait()
        for s in range(2):
            @pl.when(nslot == s)
            def _():
                issue_rows(s)

        @pl.when(i + 2 < n)
        def _():
            pos_copy(i + 2, slot).start()

    wait_rows(slot)
    tw = tw_ref[...]
    y = x1_ref[...]
    for kk in range(TOP_K):
        y = y + tw[:, kk:kk + 1] * gbuf[slot, kk]

    @pl.when(i < n_prompt_tiles)
    def _():
        yp_ref[...] = y

    @pl.when(i >= n_prompt_tiles)
    def _():
        ys_ref[...] = y


def _combine(x1, expert_rows, dest, top_w, tp, ts):
    t = tp + ts
    tile = _pick_tile(int(np.gcd(tp, ts)), COMBINE_ROWS)
    assert tile == COMBINE_ROWS
    npt, nst = tp // tile, ts // tile
    p_spec, s_spec = _two_group_specs(tile, npt, nst, D_MODEL)
    return pl.pallas_call(
        functools.partial(_combine_kernel, n_prompt_tiles=npt),
        grid=(npt + nst,),
        in_specs=[pl.BlockSpec(memory_space=pl.ANY), pl.BlockSpec(memory_space=pl.ANY),
                  pl.BlockSpec((tile, D_MODEL), lambda m: (m, 0)), pl.BlockSpec((tile, LANES), lambda m: (m, 0))],
        out_specs=[p_spec, s_spec],
        out_shape=[jax.ShapeDtypeStruct((tp, D_MODEL), F32), jax.ShapeDtypeStruct((ts, D_MODEL), F32)],
        scratch_shapes=[
            pltpu.SMEM((2, COMBINE_ROWS * TOP_K), jnp.int32),
            pltpu.VMEM((2, TOP_K, COMBINE_ROWS, D_MODEL), F32),
            pltpu.SemaphoreType.DMA((2,)),
            pltpu.SemaphoreType.DMA((2,)),
        ],
        compiler_params=_cparams(("arbitrary",), 32),
        name="moe_combine",
    )(dest.reshape(t // tile, tile * TOP_K), expert_rows, x1, top_w)


def _sequence_flags(seq_lengths, unit):
    first, last = [], []
    for length in seq_lengths:
        n = length // unit
        first += [1] + [0] * (n - 1)
        last += [0] * (n - 1) + [1]
    return jnp.asarray(np.array([first, last], np.int32))


def _in_projections(h, w_in, q_norm_w, k_norm_w):
    w = w_in.astype(BF16)
    offs = np.cumsum([0, N_HEADS * HEAD_DIM, N_KV_HEADS * HEAD_DIM, N_KV_HEADS * HEAD_DIM, D_INNER, C_XBC,
                      2 * SSD_HEADS, D_MODEL, D_MODEL])
    seg = [w[:, offs[i]:offs[i + 1]] for i in range(8)]
    w_dt = jnp.pad(seg[5], ((0, 0), (0, LANES - 2 * SSD_HEADS)))

    def head_w_spec(tm, tn):
        return [pl.BlockSpec((1, HEAD_DIM), lambda m, j: (0, 0))]

    qw = (q_norm_w.astype(F32) * (LOG2_E / np.sqrt(HEAD_DIM))).reshape(1, HEAD_DIM)
    kw = k_norm_w.astype(F32).reshape(1, HEAD_DIM)
    out = {}
    tm, tn = LINEAR_TILE_M, LINEAR_TILE_N
    out['q'] = _linear([h], [seg[0]], [qw], head_w_spec, _ep_head_norm, BF16, tm, tn, "proj_q")
    out['k'] = _linear([h], [seg[1]], [kw], head_w_spec, _ep_head_norm, BF16, tm, tn, "proj_k")
    out['v'] = _linear([h], [seg[2]], [], _no_aux, _ep_cast, BF16, tm, tn, "proj_v")
    out['silu_z'] = _linear([h], [seg[3]], [], _no_aux, _ep_silu, BF16, tm, tn, "proj_z")
    out['xbc'] = _linear([h], [seg[4]], [], _no_aux, _ep_cast, BF16, tm, tn, "proj_xbc")
    out['dt'] = _linear([h], [w_dt], [], _no_aux, _ep_cast, F32, tm, tn, "proj_dt")
    out['gate_attn'] = _linear([h], [seg[6]], [], _no_aux, _ep_sigmoid, BF16, tm, tn, "proj_gate_attn")
    out['gate_ssd'] = _linear([h], [seg[7]], [], _no_aux, _ep_sigmoid, BF16, tm, tn, "proj_gate_ssd")
    return out


def _pre_attention(x_prompt, x_sample, p):
    xp = x_prompt.reshape(-1, D_MODEL)
    xs = x_sample.reshape(-1, D_MODEL)
    seq_lengths = [x_prompt.shape[1]] * x_prompt.shape[0] + [x_sample.shape[1]] * x_sample.shape[0]
    h = _norm1(xp, xs, p['norm1_w'][0])
    proj = _in_projections(h, p['w_in'][0], p['q_norm_w'][0], p['k_norm_w'][0])
    tq = _pick_tile(int(np.gcd.reduce(seq_lengths)), ATTN_CHUNK)
    flags = _sequence_flags(seq_lengths, tq)
    proj['attn'] = _attention(proj['q'], proj['k'], proj['v'], p['attn_sink'][0], flags, tq)
    proj['seq_lengths'] = seq_lengths
    if 'conv_w' in p:
        proj['ssd'] = _ssd(proj, seq_lengths, p['conv_w'][0], p['conv_b'][0], p['dt_bias'][0], p['a_log'][0],
                           p['d_skip'][0], p['ssd_norm_w'][0])
    return proj


def kernel(x_prompt, x_sample, norm1_w, w_in, q_norm_w, k_norm_w, attn_sink, conv_w, conv_b, dt_bias, a_log, d_skip, ssd_norm_w, w_attn_proj, w_ssd_proj, w_out, norm2_w, router_w, router_b, w_gate_up, b_gate_up, w_down, b_down):
    assert norm1_w.shape[0] == 1, "single-layer block"
    p = dict(norm1_w=norm1_w, w_in=w_in, q_norm_w=q_norm_w, k_norm_w=k_norm_w, attn_sink=attn_sink,
             conv_w=conv_w, conv_b=conv_b, dt_bias=dt_bias, a_log=a_log, d_skip=d_skip, ssd_norm_w=ssd_norm_w)
    xp = x_prompt.reshape(-1, D_MODEL)
    xs = x_sample.reshape(-1, D_MODEL)
    tp, ts = xp.shape[0], xs.shape[0]
    t = tp + ts
    pre = _pre_attention(x_prompt, x_sample, p)

    def gate_specs(tm, tn):
        return [pl.BlockSpec((tm, tn), lambda m, j: (m, j))] * 2

    merged = _linear([pre['attn'], pre['ssd']], [w_attn_proj[0].astype(BF16), w_ssd_proj[0].astype(BF16)],
                     [pre['gate_attn'], pre['gate_ssd']], gate_specs, _ep_gated_sum, BF16, LINEAR_TILE_M,
                     LINEAR_TILE_N, "branch_merge")
    x1, h2, top_idx, top_w, counts = _out_router(merged, w_out[0], xp, xs, norm2_w[0], router_w[0], router_b[0])

    n_blocks = -(-(t * TOP_K) // MOE_ROWS) + N_EXPERTS
    tok_sorted, block_expert, n_used, dest = _route(top_idx[:, :TOP_K], top_idx[:, RANK_LANE:RANK_LANE + TOP_K],
                                                    counts[0, :N_EXPERTS].astype(jnp.int32), n_blocks)
    expert_rows = _moe(h2, tok_sorted, block_expert, n_used, w_gate_up[0], b_gate_up[0], w_down[0], b_down[0])
    yp, ys = _combine(x1, expert_rows, dest, top_w, tp, ts)
    return yp.reshape(x_prompt.shape), ys.reshape(x_sample.shape)
```

```python
import functools

import numpy as np
import jax
import jax.numpy as jnp
from jax import lax
from jax.experimental import pallas as pl
from jax.experimental.pallas import tpu as pltpu

F32 = jnp.float32
BF16 = jnp.bfloat16
HIGHEST = lax.Precision.HIGHEST

D_MODEL = 2048
N_HEADS = 16
N_KV_HEADS = 4
Q_PER_KV = N_HEADS // N_KV_HEADS
HEAD_DIM = 128
WINDOW = 128
ATTN_BLOCK = 128
D_INNER = 2048
SSD_HEAD_DIM = 64
SSD_HEADS = D_INNER // SSD_HEAD_DIM
SSD_GROUPS = 4
HEADS_PER_GROUP = SSD_HEADS // SSD_GROUPS
GROUP_WIDTH = D_INNER // SSD_GROUPS
D_STATE = 128
CONV_K = 5
SSD_CHUNK = 128
C_XBC = D_INNER + 2 * SSD_GROUPS * D_STATE
N_EXPERTS = 32
TOP_K = 4
D_FF = D_MODEL
SWIGLU_LIMIT = 7.0
SWIGLU_ALPHA = 1.702
NORM_EPS = 1e-6
MASK_VALUE = -1e30
LOG2_E = float(np.log2(np.e))

LANES = 128
BF16_SUBLANES = 16
MIB = 1 << 20


def _cparams(semantics, vmem_mib):
    return pltpu.CompilerParams(dimension_semantics=semantics, vmem_limit_bytes=vmem_mib * MIB)


def _sigmoid(x):
    return 1.0 / (1.0 + jnp.exp(-x))


def _pick_tile(total, preferred):
    t = min(total, preferred)
    while total % t:
        t //= 2
    return t


def _norm1_kernel(xp_ref, xs_ref, w_ref, o_ref, *, n_prompt_tiles):
    m = pl.program_id(0)

    def body(x_ref):
        x = x_ref[...]
        ms = jnp.mean(x * x, axis=-1, keepdims=True)
        o_ref[...] = (x * lax.rsqrt(ms + NORM_EPS) * w_ref[...]).astype(BF16)

    @pl.when(m < n_prompt_tiles)
    def _():
        body(xp_ref)

    @pl.when(m >= n_prompt_tiles)
    def _():
        body(xs_ref)


def _two_group_specs(tile, n_prompt_tiles, n_sample_tiles, width):
    p_spec = pl.BlockSpec((tile, width), lambda m: (jnp.minimum(m, n_prompt_tiles - 1), 0))
    s_spec = pl.BlockSpec((tile, width), lambda m: (jnp.maximum(m - n_prompt_tiles, 0), 0))
    return p_spec, s_spec


def _norm1(xp, xs, w):
    tp, ts = xp.shape[0], xs.shape[0]
    tile = _pick_tile(int(np.gcd(tp, ts)), 512)
    npt, nst = tp // tile, ts // tile
    p_spec, s_spec = _two_group_specs(tile, npt, nst, D_MODEL)
    return pl.pallas_call(
        functools.partial(_norm1_kernel, n_prompt_tiles=npt),
        grid=(npt + nst,),
        in_specs=[p_spec, s_spec, pl.BlockSpec((1, D_MODEL), lambda m: (0, 0))],
        out_specs=pl.BlockSpec((tile, D_MODEL), lambda m: (m, 0)),
        out_shape=jax.ShapeDtypeStruct((tp + ts, D_MODEL), BF16),
        compiler_params=_cparams(("parallel",), 40),
        name="norm1",
    )(xp, xs, w.reshape(1, D_MODEL))


LINEAR_TILE_M = 1024
LINEAR_TILE_N = 1024


def _linear_kernel(*refs, n_lhs, n_aux, epilogue):
    lhs = refs[:n_lhs]
    rhs = refs[n_lhs:2 * n_lhs]
    aux = refs[2 * n_lhs:2 * n_lhs + n_aux]
    out = refs[2 * n_lhs + n_aux]
    accs = [jnp.dot(l[...], r[...], preferred_element_type=F32) for l, r in zip(lhs, rhs)]
    epilogue(accs, aux, out)


def _ep_cast(accs, aux, out):
    out[...] = accs[0].astype(out.dtype)


def _ep_silu(accs, aux, out):
    a = accs[0]
    out[...] = (a * _sigmoid(a)).astype(out.dtype)


def _ep_sigmoid(accs, aux, out):
    out[...] = _sigmoid(accs[0]).astype(out.dtype)


def _ep_head_norm(accs, aux, out):
    a = accs[0]
    w = aux[0][...]
    for j in range(a.shape[1] // HEAD_DIM):
        s = a[:, j * HEAD_DIM:(j + 1) * HEAD_DIM]
        ms = jnp.mean(s * s, axis=-1, keepdims=True)
        out[:, j * HEAD_DIM:(j + 1) * HEAD_DIM] = (s * lax.rsqrt(ms + NORM_EPS) * w).astype(out.dtype)


def _ep_gated_sum(accs, aux, out):
    out[...] = (aux[0][...].astype(F32) * accs[0] + aux[1][...].astype(F32) * accs[1]).astype(out.dtype)


def _linear(lhs_list, rhs_list, aux_list, aux_specs, epilogue, out_dtype, tm, tn, name):
    t, k = lhs_list[0].shape
    n = rhs_list[0].shape[1]
    tm = _pick_tile(t, tm)
    tn = _pick_tile(n, tn)
    in_specs = ([pl.BlockSpec((tm, k), lambda m, j: (m, 0)) for _ in lhs_list]
                + [pl.BlockSpec((k, tn), lambda m, j: (0, j)) for _ in rhs_list]
                + list(aux_specs(tm, tn)))
    return pl.pallas_call(
        functools.partial(_linear_kernel, n_lhs=len(lhs_list), n_aux=len(aux_list), epilogue=epilogue),
        grid=(t // tm, n // tn),
        in_specs=in_specs,
        out_specs=pl.BlockSpec((tm, tn), lambda m, j: (m, j)),
        out_shape=jax.ShapeDtypeStruct((t, n), out_dtype),
        compiler_params=_cparams(("parallel", "arbitrary"), 56),
        name=name,
    )(*lhs_list, *rhs_list, *aux_list)


def _no_aux(tm, tn):
    return []


ATTN_CHUNK = 1024


def _attn_kernel(flags_ref, slope_ref, sink_ref, q_ref, k_ref, kp_ref, kn_ref, v_ref, vp_ref, vn_ref, o_ref,
                 *, n_sub):
    i = pl.program_id(0)
    g = pl.program_id(1)
    has_prev = flags_ref[0, i] == 0
    has_next = flags_ref[1, i] == 0

    qi = lax.broadcasted_iota(jnp.int32, (ATTN_BLOCK, 3 * ATTN_BLOCK), 0)
    kj = lax.broadcasted_iota(jnp.int32, (ATTN_BLOCK, 3 * ATTN_BLOCK), 1)
    dist = jnp.abs(ATTN_BLOCK + qi - kj)
    in_window = dist <= WINDOW
    dist_f = dist.astype(F32)
    is_prev_blk = kj < ATTN_BLOCK
    is_next_blk = kj >= 2 * ATTN_BLOCK

    for j in range(n_sub):
        rows = slice(j * ATTN_BLOCK, (j + 1) * ATTN_BLOCK)
        prev_rows = slice((j - 1) * ATTN_BLOCK, j * ATTN_BLOCK)
        next_rows = slice((j + 1) * ATTN_BLOCK, (j + 2) * ATTN_BLOCK)
        k_prev = kp_ref[...] if j == 0 else k_ref[prev_rows, :]
        v_prev = vp_ref[...] if j == 0 else v_ref[prev_rows, :]
        k_next = kn_ref[...] if j == n_sub - 1 else k_ref[next_rows, :]
        v_next = vn_ref[...] if j == n_sub - 1 else v_ref[next_rows, :]
        k_band = jnp.concatenate([k_prev, k_ref[rows, :], k_next], axis=0)
        v_band = jnp.concatenate([v_prev, v_ref[rows, :], v_next], axis=0)
        valid = in_window
        if j == 0:
            valid = valid & (has_prev | jnp.logical_not(is_prev_blk))
        if j == n_sub - 1:
            valid = valid & (has_next | jnp.logical_not(is_next_blk))
        q_stack = jnp.concatenate(
            [q_ref[rows, r * HEAD_DIM:(r + 1) * HEAD_DIM] for r in range(Q_PER_KV)], axis=0)
        s_all = lax.dot_general(q_stack, k_band, (((1,), (1,)), ((), ())), preferred_element_type=F32)
        for r in range(Q_PER_KV):
            head = g * Q_PER_KV + r
            s = s_all[r * ATTN_BLOCK:(r + 1) * ATTN_BLOCK, :] - slope_ref[head] * dist_f
            s = jnp.where(valid, s, MASK_VALUE)
            sink = sink_ref[head]
            m = jnp.maximum(jnp.max(s, axis=-1, keepdims=True), sink)
            p = jnp.exp2(s - m)
            denom = jnp.sum(p, axis=-1, keepdims=True) + jnp.exp2(sink - m)
            o = jnp.dot(p.astype(BF16), v_band, preferred_element_type=F32)
            o_ref[rows, r * HEAD_DIM:(r + 1) * HEAD_DIM] = (o / denom).astype(o_ref.dtype)


def _attention(q, k, v, sink, chunk_flags, tq):
    t = q.shape[0]
    n_chunks = t // tq
    n_blocks = t // ATTN_BLOCK
    sub = tq // ATTN_BLOCK
    slopes = jnp.asarray(LOG2_E * 2.0 ** (-8.0 * (np.arange(N_HEADS, dtype=np.float32) + 1.0) / N_HEADS), F32)
    gw = Q_PER_KV * HEAD_DIM

    def own(width):
        return pl.BlockSpec((tq, width), lambda i, g, *_: (i, g))

    prev = pl.BlockSpec((ATTN_BLOCK, HEAD_DIM), lambda i, g, *_: (jnp.maximum(i * sub - 1, 0), g))
    nxt = pl.BlockSpec((ATTN_BLOCK, HEAD_DIM), lambda i, g, *_: (jnp.minimum((i + 1) * sub, n_blocks - 1), g))
    grid_spec = pltpu.PrefetchScalarGridSpec(
        num_scalar_prefetch=3,
        grid=(n_chunks, N_KV_HEADS),
        in_specs=[own(gw), own(HEAD_DIM), prev, nxt, own(HEAD_DIM), prev, nxt],
        out_specs=own(gw),
    )
    return pl.pallas_call(
        functools.partial(_attn_kernel, n_sub=sub),
        grid_spec=grid_spec,
        out_shape=jax.ShapeDtypeStruct((t, N_HEADS * HEAD_DIM), BF16),
        compiler_params=_cparams(("parallel", "arbitrary"), 32),
        name="banded_attention",
    )(chunk_flags, slopes, sink.astype(F32) * LOG2_E, q, k, k, k, v, v, v)


CONV_HALO = BF16_SUBLANES
CONV_PAD = CONV_K // 2
CONV_TAP_ROWS = 8
CONV_COL_TILE = 512
HEAD_PAIR_WIDTH = 2 * SSD_HEAD_DIM


def _ssd_kernel(flags_ref, *rest, reverse):
    i = pl.program_id(0)
    c = pl.num_programs(0) - 1 - i if reverse else i
    seq_first = flags_ref[0, c] == 1
    seq_last = flags_ref[1, c] == 1
    L = SSD_CHUNK

    if reverse:
        xc_ref, dt_ref, dtb_ref, alog_ref, yf_ref, sz_ref, dskip_ref, normw_ref, out_ref, y_scr, state_scr = rest

        def xc(cols):
            return xc_ref[:, cols].astype(F32)
    else:
        (xbc_ref, xprev_ref, xnext_ref, dt_ref, convw_ref, convb_ref, dtb_ref, alog_ref, out_ref, xc_out_ref,
         ext_scr, xc_scr, y_scr, state_scr) = rest

        def xc(cols):
            return xc_scr[:, cols]

        zero_halo = jnp.zeros((CONV_HALO, C_XBC), BF16)
        ext_scr[0:CONV_HALO, :] = jnp.where(seq_first, zero_halo, xprev_ref[...])
        ext_scr[CONV_HALO:CONV_HALO + L, :] = xbc_ref[...]
        ext_scr[CONV_HALO + L:, :] = jnp.where(seq_last, zero_halo, xnext_ref[...])
        out_row = lax.broadcasted_iota(jnp.int32, (L, L + 2 * CONV_HALO), 0)
        src_row = lax.broadcasted_iota(jnp.int32, (L, L + 2 * CONV_HALO), 1)
        for ct in range(C_XBC // CONV_COL_TILE):
            cols = slice(ct * CONV_COL_TILE, (ct + 1) * CONV_COL_TILE)
            ext = ext_scr[:, cols]
            acc = convb_ref[:, cols] + xbc_ref[:, cols].astype(F32) * convw_ref[CONV_PAD:CONV_PAD + 1, cols]
            for j in range(CONV_K):
                if j == CONV_PAD:
                    continue
                shift = (src_row == out_row + (CONV_HALO - CONV_PAD + j)).astype(BF16)
                acc = acc + jnp.dot(shift, ext, preferred_element_type=F32) * convw_ref[j:j + 1, cols]
            conv = acc * _sigmoid(acc)
            xc_scr[:, cols] = conv
            xc_out_ref[:, cols] = conv.astype(xc_out_ref.dtype)

    col = lax.broadcasted_iota(jnp.int32, (1, LANES), 1)
    a_neg = jnp.where(col < 2 * SSD_HEADS, -jnp.exp(alog_ref[...]), 0.0)
    xdt = dt_ref[...] + dtb_ref[...]
    dt = jnp.maximum(xdt, 0.0) + jnp.log1p(jnp.exp(-jnp.abs(xdt)))
    a = dt * (a_neg * LOG2_E)
    ri = lax.broadcasted_iota(jnp.int32, (L, L), 0)
    ci = lax.broadcasted_iota(jnp.int32, (L, L), 1)
    causal = (ri <= ci) if reverse else (ri >= ci)
    a_cum = jnp.dot(causal.astype(F32), a, precision=HIGHEST, preferred_element_type=F32)
    a_cum_t = a_cum.T
    dt_t = dt.T
    edge = 0 if reverse else L - 1
    a_total = a_cum[edge:edge + 1, :]
    w_state = dt * jnp.exp2(a_total - a_cum)
    dir_off = SSD_HEADS if reverse else 0
    hr = lax.broadcasted_iota(jnp.int32, (LANES, D_INNER), 0)
    hc = lax.broadcasted_iota(jnp.int32, (LANES, D_INNER), 1)
    expand = (hr == dir_off + lax.shift_right_logical(hc, int(np.log2(SSD_HEAD_DIM)))).astype(F32)
    chunk_decay = jnp.dot(jnp.broadcast_to(jnp.exp2(a_total), (8, LANES)), expand, precision=HIGHEST,
                          preferred_element_type=F32)[0:1, :]

    @pl.when(seq_last if reverse else seq_first)
    def _():
        state_scr[...] = jnp.zeros_like(state_scr)

    lo = lax.broadcasted_iota(jnp.int32, (L, LANES), 1) < SSD_HEAD_DIM
    for g in range(SSD_GROUPS):
        b_g = xc(slice(D_INNER + g * D_STATE, D_INNER + (g + 1) * D_STATE))
        c_g = xc(slice(D_INNER + (SSD_GROUPS + g) * D_STATE, D_INNER + (SSD_GROUPS + g + 1) * D_STATE))
        c_bf = c_g.astype(BF16)
        cb = lax.dot_general(c_bf, b_g.astype(BF16), (((1,), (1,)), ((), ())), preferred_element_type=F32)
        b_t = b_g.T.astype(BF16)
        state = state_scr[g]
        y_off = jnp.dot(c_bf, state.astype(BF16), preferred_element_type=F32)
        xw_parts = []
        for p in range(HEADS_PER_GROUP // 2):
            gcols = slice(g * GROUP_WIDTH + p * HEAD_PAIR_WIDTH, g * GROUP_WIDTH + (p + 1) * HEAD_PAIR_WIDTH)
            m_parts, e_parts, w_parts = [], [], []
            for hh in range(2):
                k = dir_off + g * HEADS_PER_GROUP + 2 * p + hh
                colb = jnp.broadcast_to(a_cum[:, k:k + 1], (L, L))
                rowb = jnp.broadcast_to(a_cum_t[k:k + 1, :], (L, L))
                dtrow = jnp.broadcast_to(dt_t[k:k + 1, :], (L, L))
                decay = jnp.exp2(jnp.where(causal, colb - rowb, -jnp.inf))
                m_parts.append((cb * decay * dtrow).astype(BF16))
                e_parts.append(jnp.exp2(colb))
                w_parts.append(jnp.broadcast_to(w_state[:, k:k + 1], (L, LANES)))
            x_pair = xc(gcols)
            rhs = jnp.concatenate([jnp.where(lo, x_pair, 0.0), jnp.where(lo, 0.0, x_pair)], axis=0).astype(BF16)
            y = jnp.dot(jnp.concatenate(m_parts, axis=1), rhs, preferred_element_type=F32)
            y = y + y_off[:, p * HEAD_PAIR_WIDTH:(p + 1) * HEAD_PAIR_WIDTH] * jnp.where(lo, e_parts[0], e_parts[1])
            y_scr[:, gcols] = y
            xw_parts.append((x_pair * jnp.where(lo, w_parts[0], w_parts[1])).astype(BF16))
        xw = jnp.concatenate(xw_parts, axis=1)
        state_scr[g] = (state * chunk_decay[:, g * GROUP_WIDTH:(g + 1) * GROUP_WIDTH]
                        + jnp.dot(b_t, xw, preferred_element_type=F32))

    if not reverse:
        out_ref[...] = y_scr[...].astype(out_ref.dtype)
    else:
        for g in range(SSD_GROUPS):
            cols = slice(g * GROUP_WIDTH, (g + 1) * GROUP_WIDTH)
            y = y_scr[:, cols] + yf_ref[:, cols].astype(F32) + xc(cols) * dskip_ref[:, cols]
            yg = y * sz_ref[:, cols].astype(F32)
            ms = jnp.mean(yg * yg, axis=-1, keepdims=True)
            out_ref[:, cols] = (yg * lax.rsqrt(ms + NORM_EPS) * normw_ref[:, cols]).astype(out_ref.dtype)


def _ssd(proj, seq_lengths, conv_w, conv_b, dt_bias, a_log, d_skip, ssd_norm_w):
    xbc, dt = proj['xbc'], proj['dt']
    t = xbc.shape[0]
    n_chunks = t // SSD_CHUNK
    halo_per_chunk = SSD_CHUNK // CONV_HALO
    n_halo_blocks = t // CONV_HALO
    flags = _sequence_flags(seq_lengths, SSD_CHUNK)
    pad = LANES - 2 * SSD_HEADS
    dtb = jnp.pad(dt_bias.astype(F32).reshape(1, -1), ((0, 0), (0, pad)))
    alog = jnp.pad(a_log.astype(F32).reshape(1, -1), ((0, 0), (0, pad)))
    scan_scratch = [pltpu.VMEM((SSD_CHUNK, D_INNER), F32), pltpu.VMEM((SSD_GROUPS, D_STATE, GROUP_WIDTH), F32)]

    def const(shape):
        return pl.BlockSpec(shape, lambda i, *_: (0, 0))

    def row(width):
        return pl.BlockSpec((SSD_CHUNK, width), lambda i, *_: (i, 0))

    def rev_row(width):
        return pl.BlockSpec((SSD_CHUNK, width), lambda i, *_: (n_chunks - 1 - i, 0))

    prev = pl.BlockSpec((CONV_HALO, C_XBC), lambda i, *_: (jnp.maximum(i * halo_per_chunk - 1, 0), 0))
    nxt = pl.BlockSpec((CONV_HALO, C_XBC),
                       lambda i, *_: (jnp.minimum((i + 1) * halo_per_chunk, n_halo_blocks - 1), 0))
    y_fwd, xc = pl.pallas_call(
        functools.partial(_ssd_kernel, reverse=False),
        grid_spec=pltpu.PrefetchScalarGridSpec(
            num_scalar_prefetch=1,
            grid=(n_chunks,),
            in_specs=[row(C_XBC), prev, nxt, row(LANES), const((CONV_TAP_ROWS, C_XBC)), const((1, C_XBC)),
                      const((1, LANES)), const((1, LANES))],
            out_specs=[row(D_INNER), row(C_XBC)],
            scratch_shapes=[pltpu.VMEM((SSD_CHUNK + 2 * CONV_HALO, C_XBC), BF16),
                            pltpu.VMEM((SSD_CHUNK, C_XBC), F32)] + scan_scratch,
        ),
        out_shape=[jax.ShapeDtypeStruct((t, D_INNER), BF16), jax.ShapeDtypeStruct((t, C_XBC), BF16)],
        compiler_params=_cparams(("arbitrary",), 40),
        name="ssd_fwd",
    )(flags, xbc, xbc, xbc, dt, jnp.pad(conv_w.astype(F32), ((0, CONV_TAP_ROWS - CONV_K), (0, 0))),
      conv_b.astype(F32).reshape(1, C_XBC), dtb, alog)

    d_lanes = jnp.repeat(d_skip.astype(F32), SSD_HEAD_DIM).reshape(1, D_INNER)
    return pl.pallas_call(
        functools.partial(_ssd_kernel, reverse=True),
        grid_spec=pltpu.PrefetchScalarGridSpec(
            num_scalar_prefetch=1,
            grid=(n_chunks,),
            in_specs=[rev_row(C_XBC), rev_row(LANES), const((1, LANES)), const((1, LANES)), rev_row(D_INNER),
                      rev_row(D_INNER), const((1, D_INNER)), const((1, D_INNER))],
            out_specs=rev_row(D_INNER),
            scratch_shapes=scan_scratch,
        ),
        out_shape=jax.ShapeDtypeStruct((t, D_INNER), BF16),
        compiler_params=_cparams(("arbitrary",), 40),
        name="ssd_bwd",
    )(flags, xc, dt, dtb, alog, y_fwd, proj['silu_z'], d_lanes, ssd_norm_w.astype(F32).reshape(1, D_INNER))


RANK_LANE = TOP_K
ROUTER_TILE = 512
ROUTER_PART = 256


def _out_router_kernel(m_ref, w_ref, xp_ref, xs_ref, nw_ref, rw_ref, rb_ref, x1_ref, h2_ref, ti_ref, tw_ref,
                       cnt_ref, *, n_prompt_tiles):
    i = pl.program_id(0)

    @pl.when(i == 0)
    def _():
        cnt_ref[...] = jnp.zeros_like(cnt_ref)

    def finish(rows):
        acc = jnp.dot(m_ref[rows, :], w_ref[...], preferred_element_type=F32)
        x1 = jnp.where(i < n_prompt_tiles, xp_ref[rows, :], xs_ref[rows, :]) + acc
        x1_ref[rows, :] = x1
        ms = jnp.mean(x1 * x1, axis=-1, keepdims=True)
        h2 = x1 * lax.rsqrt(ms + NORM_EPS) * nw_ref[...]
        h2_ref[rows, :] = h2
        v = jnp.dot(h2.astype(BF16), rw_ref[...], preferred_element_type=F32) + rb_ref[...]
        lane = lax.broadcasted_iota(jnp.int32, v.shape, 1)
        vals, idxs = [], []
        for _ in range(TOP_K):
            top = jnp.max(v, axis=-1, keepdims=True)
            idx = jnp.min(jnp.where(v == top, lane, LANES), axis=-1, keepdims=True)
            vals.append(top)
            idxs.append(idx)
            v = jnp.where(lane == idx, -jnp.inf, v)
        es = [jnp.exp(val - vals[0]) for val in vals]
        total = es[0]
        for e in es[1:]:
            total = total + e
        n_rows = v.shape[0]
        chosen = jnp.zeros(v.shape, F32)
        for kk in range(TOP_K):
            chosen = chosen + (lane == idxs[kk]).astype(F32)
        ri = lax.broadcasted_iota(jnp.int32, (n_rows, n_rows), 0)
        ci = lax.broadcasted_iota(jnp.int32, (n_rows, n_rows), 1)
        before = jnp.dot((ci < ri).astype(BF16), chosen.astype(BF16), preferred_element_type=F32)
        before = before + cnt_ref[0:1, :]
        cnt_ref[...] = cnt_ref[...] + jnp.sum(chosen, axis=0, keepdims=True)
        ti = jnp.zeros(v.shape, jnp.int32)
        tw = jnp.zeros(v.shape, F32)
        for kk in range(TOP_K):
            rank = jnp.sum(jnp.where(lane == idxs[kk], before, 0.0), axis=-1, keepdims=True)
            ti = jnp.where(lane == kk, idxs[kk], ti)
            ti = jnp.where(lane == RANK_LANE + kk, rank.astype(jnp.int32), ti)
            tw = jnp.where(lane == kk, es[kk] / total, tw)
        ti_ref[rows, :] = ti
        tw_ref[rows, :] = tw

    for part in range(m_ref.shape[0] // ROUTER_PART):
        finish(slice(part * ROUTER_PART, (part + 1) * ROUTER_PART))


def _out_router(merged, w_out, xp, xs, norm2_w, router_w, router_b):
    tp, ts = xp.shape[0], xs.shape[0]
    t = tp + ts
    tile = _pick_tile(int(np.gcd(tp, ts)), ROUTER_TILE)
    npt, nst = tp // tile, ts // tile
    p_spec, s_spec = _two_group_specs(tile, npt, nst, D_MODEL)
    rw = jnp.pad(router_w.astype(BF16), ((0, 0), (0, LANES - N_EXPERTS)))
    rb = jnp.pad(router_b.astype(F32).reshape(1, N_EXPERTS), ((0, 0), (0, LANES - N_EXPERTS)),
                 constant_values=-jnp.inf)

    def const(shape):
        return pl.BlockSpec(shape, lambda m: (0, 0), pipeline_mode=pl.Buffered(1))

    def row(width):
        return pl.BlockSpec((tile, width), lambda m: (m, 0))

    return pl.pallas_call(
        functools.partial(_out_router_kernel, n_prompt_tiles=npt),
        grid=(npt + nst,),
        in_specs=[row(D_MODEL), const((D_MODEL, D_MODEL)), p_spec, s_spec, const((1, D_MODEL)),
                  const((D_MODEL, LANES)), const((1, LANES))],
        out_specs=[row(D_MODEL), row(D_MODEL), row(LANES), row(LANES), pl.BlockSpec((8, LANES), lambda m: (0, 0))],
        out_shape=[jax.ShapeDtypeStruct((t, D_MODEL), F32), jax.ShapeDtypeStruct((t, D_MODEL), F32),
                   jax.ShapeDtypeStruct((t, LANES), jnp.int32), jax.ShapeDtypeStruct((t, LANES), F32),
                   jax.ShapeDtypeStruct((8, LANES), F32)],
        compiler_params=_cparams(("arbitrary",), 56),
        name="out_proj_router",
    )(merged, w_out.astype(BF16), xp, xs, norm2_w.astype(F32).reshape(1, D_MODEL), rw, rb)


MOE_ROWS = 512
MOE_FF_TILE = 1024
MOE_GU_CHUNK = 128
MOE_D_CHUNK = 256
MOE_VMEM_MIB = 60


def _moe_kernel(be_ref, nb_ref, tok_hbm, h2_hbm, wgu_hbm, wd_hbm, bgu_ref, bd_ref, out_ref,
                idx_smem, xbuf, xb_scr, wgu_bf, wd_bf, stage_gu, stage_d, idx_sem, row_sem, w_sem):
    b = pl.program_id(0)
    n_used = nb_ref[0]
    expert = be_ref[b]
    expert_changed = jnp.logical_or(b == 0, expert != be_ref[jnp.maximum(b - 1, 0)])

    n_gu = D_MODEL // MOE_GU_CHUNK
    n_d = D_FF // MOE_D_CHUNK

    def weight_copy(k, slot):
        if k < n_gu:
            return pltpu.make_async_copy(wgu_hbm.at[expert, pl.ds(k * MOE_GU_CHUNK, MOE_GU_CHUNK), :],
                                         stage_gu.at[slot], w_sem.at[slot])
        k -= n_gu
        return pltpu.make_async_copy(wd_hbm.at[expert, pl.ds(k * MOE_D_CHUNK, MOE_D_CHUNK), :],
                                     stage_d.at[slot], w_sem.at[2 + slot])

    def start_expert_load():
        weight_copy(0, 0).start(priority=1)
        weight_copy(1, 1).start(priority=1)

    def finish_expert_load():
        for k in range(n_gu + n_d):
            slot = k % 2
            weight_copy(k, slot).wait()
            if k < n_gu:
                wgu_bf[k * MOE_GU_CHUNK:(k + 1) * MOE_GU_CHUNK, :] = stage_gu[slot].astype(BF16)
            else:
                kd = k - n_gu
                wd_bf[kd * MOE_D_CHUNK:(kd + 1) * MOE_D_CHUNK, :] = stage_d[slot].astype(BF16)
            if k + 2 < n_gu + n_d:
                weight_copy(k + 2, slot).start(priority=1)

    def idx_copy(block, slot):
        return pltpu.make_async_copy(tok_hbm.at[block], idx_smem.at[slot], idx_sem.at[slot])

    def issue_rows(slot):
        def body(r, carry):
            tok = idx_smem[slot, r]
            pltpu.make_async_copy(h2_hbm.at[pl.ds(tok, 1), :], xbuf.at[slot, pl.ds(r, 1), :],
                                  row_sem.at[slot]).start()
            return carry
        lax.fori_loop(0, MOE_ROWS, body, 0, unroll=True)

    def wait_rows(slot):
        pltpu.make_async_copy(h2_hbm.at[pl.ds(0, MOE_ROWS), :], xbuf.at[slot], row_sem.at[slot]).wait()

    @pl.when(b < n_used)
    def _():
        slot = lax.rem(b, 2)
        nslot = 1 - slot

        @pl.when(expert_changed)
        def _():
            start_expert_load()

        @pl.when(b == 0)
        def _():
            idx_copy(0, 0).start()
            idx_copy(0, 0).wait()
            issue_rows(0)

            @pl.when(n_used > 1)
            def _():
                idx_copy(1, 1).start()

        @pl.when(b + 1 < n_used)
        def _():
            idx_copy(b + 1, nslot).wait()
            for s in range(2):
                @pl.when(nslot == s)
                def _():
                    issue_rows(s)

            @pl.when(b + 2 < n_used)
            def _():
                idx_copy(b + 2, slot).start()

        @pl.when(expert_changed)
        def _():
            finish_expert_load()

        wait_rows(slot)
        xb_scr[...] = xbuf[slot].astype(BF16)

        for j in range(D_FF // MOE_FF_TILE):
            cols = slice(j * MOE_FF_TILE, (j + 1) * MOE_FF_TILE)
            up_cols = slice(D_FF + j * MOE_FF_TILE, D_FF + (j + 1) * MOE_FF_TILE)
            xb = xb_scr[...]
            gate = jnp.dot(xb, wgu_bf[:, cols], preferred_element_type=F32) + bgu_ref[0, :, cols]
            up = jnp.dot(xb, wgu_bf[:, up_cols], preferred_element_type=F32) + bgu_ref[0, :, up_cols]
            gate = jnp.minimum(gate, SWIGLU_LIMIT)
            up = jnp.clip(up, -SWIGLU_LIMIT, SWIGLU_LIMIT)
            act = (gate * _sigmoid(SWIGLU_ALPHA * gate) * (up + 1.0)).astype(BF16)
            part = jnp.dot(act, wd_bf[cols, :], preferred_element_type=F32)
            out_ref[...] = (bd_ref[0] if j == 0 else out_ref[...]) + part

    @pl.when(b >= n_used)
    def _():
        out_ref[...] = jnp.zeros_like(out_ref)


def _moe(h2, tok_sorted, block_expert, n_used, w_gate_up, b_gate_up, w_down, b_down):
    n_blocks = tok_sorted.shape[0]
    bgu = b_gate_up.astype(F32).reshape(N_EXPERTS, 1, 2 * D_FF)
    bd = b_down.astype(F32).reshape(N_EXPERTS, 1, D_MODEL)
    grid_spec = pltpu.PrefetchScalarGridSpec(
        num_scalar_prefetch=2,
        grid=(n_blocks,),
        in_specs=[
            pl.BlockSpec(memory_space=pl.ANY),
            pl.BlockSpec(memory_space=pl.ANY),
            pl.BlockSpec(memory_space=pl.ANY),
            pl.BlockSpec(memory_space=pl.ANY),
            pl.BlockSpec((1, 1, 2 * D_FF), lambda b, be, nb: (be[b], 0, 0)),
            pl.BlockSpec((1, 1, D_MODEL), lambda b, be, nb: (be[b], 0, 0)),
        ],
        out_specs=pl.BlockSpec((MOE_ROWS, D_MODEL), lambda b, be, nb: (b, 0)),
        scratch_shapes=[
            pltpu.SMEM((2, MOE_ROWS), jnp.int32),
            pltpu.VMEM((2, MOE_ROWS, D_MODEL), F32),
            pltpu.VMEM((MOE_ROWS, D_MODEL), BF16),
            pltpu.VMEM((D_MODEL, 2 * D_FF), BF16),
            pltpu.VMEM((D_FF, D_MODEL), BF16),
            pltpu.VMEM((2, MOE_GU_CHUNK, 2 * D_FF), F32),
            pltpu.VMEM((2, MOE_D_CHUNK, D_MODEL), F32),
            pltpu.SemaphoreType.DMA((2,)),
            pltpu.SemaphoreType.DMA((2,)),
            pltpu.SemaphoreType.DMA((4,)),
        ],
    )
    return pl.pallas_call(
        _moe_kernel,
        grid_spec=grid_spec,
        out_shape=jax.ShapeDtypeStruct((n_blocks * MOE_ROWS, D_MODEL), F32),
        compiler_params=_cparams(("arbitrary",), MOE_VMEM_MIB),
        name="moe_experts",
    )(block_expert, n_used, tok_sorted, h2, w_gate_up.astype(F32), w_down.astype(F32), bgu, bd)


def _route(top_idx, rank, counts, n_blocks):
    t = top_idx.shape[0]
    flat_e = top_idx.reshape(-1)
    rank = rank.reshape(-1)
    blocks_e = (counts + MOE_ROWS - 1) // MOE_ROWS
    blocks_end = jnp.cumsum(blocks_e)
    blocks_start = blocks_end - blocks_e
    dest = (blocks_start[flat_e] * MOE_ROWS + rank).astype(jnp.int32)
    n_used = blocks_end[-1].astype(jnp.int32)
    flat_tok = jnp.arange(t * TOP_K, dtype=jnp.int32) // TOP_K
    tok_sorted = jnp.zeros((n_blocks * MOE_ROWS,), jnp.int32).at[dest].set(flat_tok, unique_indices=True)
    bidx = jnp.arange(n_blocks, dtype=jnp.int32)
    be = jnp.minimum(jnp.sum((blocks_end[None, :] <= bidx[:, None]).astype(jnp.int32), axis=1), N_EXPERTS - 1)
    be = jnp.where(bidx < n_used, be, be[jnp.maximum(n_used - 1, 0)])
    return tok_sorted.reshape(n_blocks, MOE_ROWS), be, n_used.reshape(1), dest


COMBINE_ROWS = 128


def _combine_kernel(pos_hbm, rows_hbm, x1_ref, tw_ref, yp_ref, ys_ref, pos_smem, gbuf, pos_sem, row_sem,
                    *, n_prompt_tiles):
    i = pl.program_id(0)
    n = pl.num_programs(0)
    slot = lax.rem(i, 2)
    nslot = 1 - slot

    def pos_copy(tile, s):
        return pltpu.make_async_copy(pos_hbm.at[tile], pos_smem.at[s], pos_sem.at[s])

    def issue_rows(s):
        def body(r, carry):
            for kk in range(TOP_K):
                src = pos_smem[s, r * TOP_K + kk]
                pltpu.make_async_copy(rows_hbm.at[pl.ds(src, 1), :], gbuf.at[s, kk, pl.ds(r, 1), :],
                                      row_sem.at[s]).start(priority=kk % 2)
            return carry
        lax.fori_loop(0, COMBINE_ROWS, body, 0, unroll=True)

    def wait_rows(s):
        for kk in range(TOP_K):
            pltpu.make_async_copy(rows_hbm.at[pl.ds(0, COMBINE_ROWS), :], gbuf.at[s, kk], row_sem.at[s]).wait()

    @pl.when(i == 0)
    def _():
        pos_copy(0, 0).start()
        pos_copy(0, 0).wait()
        issue_rows(0)

        @pl.when(n > 1)
        def _():
            pos_copy(1, 1).start()

    @pl.when(i + 1 < n)
    def _():
        pos_copy(i + 1, nslot).wait()
        for s in range(2):
            @pl.when(nslot == s)
            def _():
                issue_rows(s)

        @pl.when(i + 2 < n)
        def _():
            pos_copy(i + 2, slot).start()

    wait_rows(slot)
    tw = tw_ref[...]
    y = x1_ref[...]
    for kk in range(TOP_K):
        y = y + tw[:, kk:kk + 1] * gbuf[slot, kk]

    @pl.when(i < n_prompt_tiles)
    def _():
        yp_ref[...] = y

    @pl.when(i >= n_prompt_tiles)
    def _():
        ys_ref[...] = y


def _combine(x1, expert_rows, dest, top_w, tp, ts):
    t = tp + ts
    tile = _pick_tile(int(np.gcd(tp, ts)), COMBINE_ROWS)
    assert tile == COMBINE_ROWS
    npt, nst = tp // tile, ts // tile
    p_spec, s_spec = _two_group_specs(tile, npt, nst, D_MODEL)
    return pl.pallas_call(
        functools.partial(_combine_kernel, n_prompt_tiles=npt),
        grid=(npt + nst,),
        in_specs=[pl.BlockSpec(memory_space=pl.ANY), pl.BlockSpec(memory_space=pl.ANY),
                  pl.BlockSpec((tile, D_MODEL), lambda m: (m, 0)), pl.BlockSpec((tile, LANES), lambda m: (m, 0))],
        out_specs=[p_spec, s_spec],
        out_shape=[jax.ShapeDtypeStruct((tp, D_MODEL), F32), jax.ShapeDtypeStruct((ts, D_MODEL), F32)],
        scratch_shapes=[
            pltpu.SMEM((2, COMBINE_ROWS * TOP_K), jnp.int32),
            pltpu.VMEM((2, TOP_K, COMBINE_ROWS, D_MODEL), F32),
            pltpu.SemaphoreType.DMA((2,)),
            pltpu.SemaphoreType.DMA((2,)),
        ],
        compiler_params=_cparams(("arbitrary",), 32),
        name="moe_combine",
    )(dest.reshape(t // tile, tile * TOP_K), expert_rows, x1, top_w)


def _sequence_flags(seq_lengths, unit):
    first, last = [], []
    for length in seq_lengths:
        n = length // unit
        first += [1] + [0] * (n - 1)
        last += [0] * (n - 1) + [1]
    return jnp.asarray(np.array([first, last], np.int32))


def _in_projections(h, w_in, q_norm_w, k_norm_w):
    w = w_in.astype(BF16)
    offs = np.cumsum([0, N_HEADS * HEAD_DIM, N_KV_HEADS * HEAD_DIM, N_KV_HEADS * HEAD_DIM, D_INNER, C_XBC,
                      2 * SSD_HEADS, D_MODEL, D_MODEL])
    seg = [w[:, offs[i]:offs[i + 1]] for i in range(8)]
    w_dt = jnp.pad(seg[5], ((0, 0), (0, LANES - 2 * SSD_HEADS)))

    def head_w_spec(tm, tn):
        return [pl.BlockSpec((1, HEAD_DIM), lambda m, j: (0, 0))]

    qw = (q_norm_w.astype(F32) * (LOG2_E / np.sqrt(HEAD_DIM))).reshape(1, HEAD_DIM)
    kw = k_norm_w.astype(F32).reshape(1, HEAD_DIM)
    out = {}
    tm, tn = LINEAR_TILE_M, LINEAR_TILE_N
    out['q'] = _linear([h], [seg[0]], [qw], head_w_spec, _ep_head_norm, BF16, tm, tn, "proj_q")
    out['k'] = _linear([h], [seg[1]], [kw], head_w_spec, _ep_head_norm, BF16, tm, tn, "proj_k")
    out['v'] = _linear([h], [seg[2]], [], _no_aux, _ep_cast, BF16, tm, tn, "proj_v")
    out['silu_z'] = _linear([h], [seg[3]], [], _no_aux, _ep_silu, BF16, tm, tn, "proj_z")
    out['xbc'] = _linear([h], [seg[4]], [], _no_aux, _ep_cast, BF16, tm, tn, "proj_xbc")
    out['dt'] = _linear([h], [w_dt], [], _no_aux, _ep_cast, F32, tm, tn, "proj_dt")
    out['gate_attn'] = _linear([h], [seg[6]], [], _no_aux, _ep_sigmoid, BF16, tm, tn, "proj_gate_attn")
    out['gate_ssd'] = _linear([h], [seg[7]], [], _no_aux, _ep_sigmoid, BF16, tm, tn, "proj_gate_ssd")
    return out


def _pre_attention(x_prompt, x_sample, p):
    xp = x_prompt.reshape(-1, D_MODEL)
    xs = x_sample.reshape(-1, D_MODEL)
    seq_lengths = [x_prompt.shape[1]] * x_prompt.shape[0] + [x_sample.shape[1]] * x_sample.shape[0]
    h = _norm1(xp, xs, p['norm1_w'][0])
    proj = _in_projections(h, p['w_in'][0], p['q_norm_w'][0], p['k_norm_w'][0])
    tq = _pick_tile(int(np.gcd.reduce(seq_lengths)), ATTN_CHUNK)
    flags = _sequence_flags(seq_lengths, tq)
    proj['attn'] = _attention(proj['q'], proj['k'], proj['v'], p['attn_sink'][0], flags, tq)
    proj['seq_lengths'] = seq_lengths
    if 'conv_w' in p:
        proj['ssd'] = _ssd(proj, seq_lengths, p['conv_w'][0], p['conv_b'][0], p['dt_bias'][0], p['a_log'][0],
                           p['d_skip'][0], p['ssd_norm_w'][0])
    return proj


def kernel(x_prompt, x_sample, norm1_w, w_in, q_norm_w, k_norm_w, attn_sink, conv_w, conv_b, dt_bias, a_log, d_skip, ssd_norm_w, w_attn_proj, w_ssd_proj, w_out, norm2_w, router_w, router_b, w_gate_up, b_gate_up, w_down, b_down):
    assert norm1_w.shape[0] == 1, "single-layer block"
    p = dict(norm1_w=norm1_w, w_in=w_in, q_norm_w=q_norm_w, k_norm_w=k_norm_w, attn_sink=attn_sink,
             conv_w=conv_w, conv_b=conv_b, dt_bias=dt_bias, a_log=a_log, d_skip=d_skip, ssd_norm_w=ssd_norm_w)
    xp = x_prompt.reshape(-1, D_MODEL)
    xs = x_sample.reshape(-1, D_MODEL)
    tp, ts = xp.shape[0], xs.shape[0]
    t = tp + ts
    pre = _pre_attention(x_prompt, x_sample, p)

    def gate_specs(tm, tn):
        return [pl.BlockSpec((tm, tn), lambda m, j: (m, j))] * 2

    merged = _linear([pre['attn'], pre['ssd']], [w_attn_proj[0].astype(BF16), w_ssd_proj[0].astype(BF16)],
                     [pre['gate_attn'], pre['gate_ssd']], gate_specs, _ep_gated_sum, BF16, LINEAR_TILE_M,
                     LINEAR_TILE_N, "branch_merge")
    x1, h2, top_idx, top_w, counts = _out_router(merged, w_out[0], xp, xs, norm2_w[0], router_w[0], router_b[0])

    n_blocks = -(-(t * TOP_K) // MOE_ROWS) + N_EXPERTS
    tok_sorted, block_expert, n_used, dest = _route(top_idx[:, :TOP_K], top_idx[:, RANK_LANE:RANK_LANE + TOP_K],
                                                    counts[0, :N_EXPERTS].astype(jnp.int32), n_blocks)
    expert_rows = _moe(h2, tok_sorted, block_expert, n_used, w_gate_up[0], b_gate_up[0], w_down[0], b_down[0])
    yp, ys = _combine(x1, expert_rows, dest, top_w, tp, ts)
    return yp.reshape(x_prompt.shape), ys.reshape(x_sample.shape)
```
